```python
import math
import jax, jax.numpy as jnp
from jax import lax
import numpy as np

D_MODEL = 1024
BATCH = 4
SEQ = 4096
DEPTH = 1
DEC_BATCH = 8
DEC_SEQ = 16
PAST_LEN = 4096

CHUNK = 64
Q_BLOCK = 128
NORM_EPS = 1e-6

DA_HEADS = 8
DA_HD = D_MODEL // (2 * DA_HEADS)
DA_VD = 2 * DA_HD
DA_QK = DA_HEADS * 2 * DA_HD
DA_V = DA_HEADS * DA_VD
ROT_DIM = DA_HD // 4
ROPE_THETA = 500000.0

GLA_HEADS = 4
GLA_K = D_MODEL // 2
GLA_V = D_MODEL
GLA_HK = GLA_K // GLA_HEADS
GLA_HV = GLA_V // GLA_HEADS
GLA_GATE_RANK = 16
GLA_GATE_TAU = 16.0

IN_SIZES = (DA_QK, DA_QK, DA_V, GLA_K, GLA_K, GLA_V, GLA_V, GLA_GATE_RANK, D_MODEL, D_MODEL)
IN_WIDTH = sum(IN_SIZES)

PEER_HEADS = 8
N_KEYS = 128
N_EXPERTS = N_KEYS * N_KEYS
PK_DIM = 128
PEER_TOPK = 16
PEER_BLOCK = 256

kernel_name = 'hybrid_stream_diffattn_gla_peer'


def rmsnorm(x, g):
    xf = x.astype(jnp.float32)
    y = xf * lax.rsqrt(jnp.mean(xf * xf, axis=-1, keepdims=True) + NORM_EPS)
    return (y * g.astype(jnp.float32)).astype(x.dtype)


def rope_partial(x, pos):
    half = ROT_DIM // 2
    inv = jnp.power(ROPE_THETA, -jnp.arange(half, dtype=jnp.float32) * 2.0 / ROT_DIM)
    ang = pos.astype(jnp.float32)[:, None] * inv[None, :]
    cos = jnp.cos(ang)[None, :, None, None, :]
    sin = jnp.sin(ang)[None, :, None, None, :]
    xf = x.astype(jnp.float32)
    x1 = xf[..., :half]
    x2 = xf[..., half:ROT_DIM]
    out = jnp.concatenate([x1 * cos - x2 * sin, x2 * cos + x1 * sin, xf[..., ROT_DIM:]], axis=-1)
    return out.astype(x.dtype)


def diff_attn_core(q, k, v, lam, mask):
    s = jnp.einsum('bqhcd,bkhcd->bhcqk', q.astype(jnp.float32), k.astype(jnp.float32)) * (DA_HD ** -0.5)
    if mask is not None:
        s = jnp.where(mask, s, -jnp.inf)
    p = jax.nn.softmax(s, axis=-1)
    w = p[:, :, 0] - lam * p[:, :, 1]
    return jnp.einsum('bhqk,bkhv->bqhv', w, v.astype(jnp.float32))


def diff_attn_prompt(q, k, v, lam):
    B, L = q.shape[0], q.shape[1]
    nb = L // Q_BLOCK
    qb = q.reshape(B, nb, Q_BLOCK, DA_HEADS, 2, DA_HD).swapaxes(0, 1)
    k_chunk = jnp.arange(L) // CHUNK

    def one(args):
        i, qi = args
        q_chunk = (i * Q_BLOCK + jnp.arange(Q_BLOCK)) // CHUNK
        mask = k_chunk[None, :] <= q_chunk[:, None]
        return diff_attn_core(qi, k, v, lam, mask)

    o = lax.map(one, (jnp.arange(nb), qb))
    return o.swapaxes(0, 1).reshape(B, L, DA_HEADS, DA_VD)


def gla_scan(q, k, v, log_a, s0, block):
    B, L = q.shape[0], q.shape[1]
    n = L // block

    def to_blocks(t):
        return t.reshape(B, n, block, *t.shape[2:]).swapaxes(0, 1)

    causal = jnp.tril(jnp.ones((block, block), dtype=bool))

    def step(S, xs):
        qc, kc, vc, ac = xs
        b = jnp.cumsum(ac, axis=1)
        b_last = b[:, -1]
        qd = qc * jnp.exp(b)
        kd = kc * jnp.exp(-b)
        att = jnp.where(causal, jnp.einsum('bchd,bshd->bhcs', qd, kd), 0.0)
        o = jnp.einsum('bhcs,bshv->bchv', att, vc) + jnp.einsum('bchd,bhdv->bchv', qd, S)
        S_new = jnp.exp(b_last)[..., None] * S + jnp.einsum(
            'bshd,bshv->bhdv', kc * jnp.exp(b_last[:, None] - b), vc)
        return S_new, o

    S_fin, o = lax.scan(step, s0, (to_blocks(q), to_blocks(k), to_blocks(v), to_blocks(log_a)))
    return o.swapaxes(0, 1).reshape(B, L, GLA_HEADS, GLA_HV), S_fin


def mixer_block(h, pos, past_k, past_v, s0, gla_block, lam_init, w_in, lq1, lk1, lq2, lk2,
                subln_g, w_a, b_a, gla_g, w_out):
    f32 = jnp.float32
    B, L, _ = h.shape
    split_points = [int(c) for c in np.cumsum(IN_SIZES)[:-1]]
    proj = h @ w_in
    dq, dk, dv, gq, gk, gv, gr, g_lr, gate_a, gate_b = jnp.split(proj, split_points, axis=-1)

    dq = rope_partial(dq.reshape(B, L, DA_HEADS, 2, DA_HD), pos)
    dk = rope_partial(dk.reshape(B, L, DA_HEADS, 2, DA_HD), pos)
    dv = dv.reshape(B, L, DA_HEADS, DA_VD)
    lam = (jnp.exp(jnp.sum(lq1.astype(f32) * lk1.astype(f32)))
           - jnp.exp(jnp.sum(lq2.astype(f32) * lk2.astype(f32))) + lam_init)
    if past_k is None:
        o_da = diff_attn_prompt(dq, dk, dv, lam)
    else:
        keys = jnp.concatenate([past_k, dk.astype(past_k.dtype)], axis=1)
        vals = jnp.concatenate([past_v, dv.astype(past_v.dtype)], axis=1)
        o_da = diff_attn_core(dq, keys, vals, lam, None)
    o_da = (rmsnorm(o_da, subln_g) * (1.0 - lam_init)).reshape(B, L, DA_V)

    log_a = jax.nn.log_sigmoid((g_lr @ w_a + b_a).astype(f32)) / GLA_GATE_TAU

    def heads(t, d):
        return t.reshape(B, L, GLA_HEADS, d).astype(f32)

    o_gla, s_fin = gla_scan(heads(gq, GLA_HK) * (GLA_HK ** -0.5), heads(gk, GLA_HK),
                            heads(gv, GLA_HV), heads(log_a, GLA_HK), s0, gla_block)
    o_gla = rmsnorm(o_gla, gla_g).reshape(B, L, GLA_V) * jax.nn.silu(gr.astype(f32))

    merged = jax.nn.sigmoid(gate_a.astype(f32)) * o_da + jax.nn.sigmoid(gate_b.astype(f32)) * o_gla
    out = merged.astype(h.dtype) @ w_out
    return out, dk, dv, s_fin


def peer(x2d, wq, subkeys, u, v):
    T = x2d.shape[0]
    blk = min(PEER_BLOCK, T)
    nb = -(-T // blk)
    xp = jnp.pad(x2d, ((0, nb * blk - T), (0, 0))).reshape(nb, blk, D_MODEL)

    def one(xb):
        q = (xb @ wq).reshape(blk, PEER_HEADS, 2, PK_DIM).astype(jnp.float32)
        s = jnp.einsum('thcd,hcnd->thcn', q, subkeys.astype(jnp.float32))
        s1, i1 = lax.top_k(s[:, :, 0], PEER_TOPK)
        s2, i2 = lax.top_k(s[:, :, 1], PEER_TOPK)
        cand = (s1[..., :, None] + s2[..., None, :]).reshape(blk, PEER_HEADS, PEER_TOPK * PEER_TOPK)
        cidx = (i1[..., :, None] * N_KEYS + i2[..., None, :]).reshape(blk, PEER_HEADS, PEER_TOPK * PEER_TOPK)
        sc, j = lax.top_k(cand, PEER_TOPK)
        idx = jnp.take_along_axis(cidx, j, axis=-1)
        g = jax.nn.softmax(sc, axis=-1)
        ue = jnp.take(u, idx, axis=0)
        ve = jnp.take(v, idx, axis=0)
        a = jax.nn.gelu(jnp.einsum('td,thkd->thk', xb.astype(jnp.float32), ue.astype(jnp.float32)),
                        approximate=False)
        return jnp.einsum('thk,thkd->td', g * a, ve.astype(jnp.float32)).astype(xb.dtype)

    return lax.map(one, xp).reshape(nb * blk, D_MODEL)[:T]


def trunk(x, pos, past_k, past_v, gla_s0, gla_block, norm1_g, w_in, da_lambda_q1, da_lambda_k1,
          da_lambda_q2, da_lambda_k2, da_subln_g, gla_w_alpha, gla_b_alpha, gla_norm_g, w_out,
          norm2_g, peer_wq, peer_subkeys, peer_u, peer_v, final_norm_g):
    B = x.shape[0]
    new_k, new_v, new_s = [], [], []
    for l in range(DEPTH):
        lam_init = 0.8 - 0.6 * math.exp(-0.3 * l)
        s0 = (jnp.zeros((B, GLA_HEADS, GLA_HK, GLA_HV), jnp.float32) if gla_s0 is None
              else gla_s0[l].astype(jnp.float32))
        pk = None if past_k is None else past_k[l]
        pv = None if past_v is None else past_v[l]
        h = rmsnorm(x, norm1_g[l])
        mix, k_l, v_l, s_l = mixer_block(h, pos, pk, pv, s0, gla_block, lam_init, w_in[l],
                                         da_lambda_q1[l], da_lambda_k1[l], da_lambda_q2[l],
                                         da_lambda_k2[l], da_subln_g[l], gla_w_alpha[l],
                                         gla_b_alpha[l], gla_norm_g[l], w_out[l])
        x = x + mix.astype(x.dtype)
        h2 = rmsnorm(x, norm2_g[l])
        x = x + peer(h2.reshape(-1, D_MODEL), peer_wq[l], peer_subkeys[l], peer_u[l],
                     peer_v[l]).reshape(x.shape)
        new_k.append(k_l)
        new_v.append(v_l)
        new_s.append(s_l)
    y = rmsnorm(x, final_norm_g)
    return y, jnp.stack(new_k), jnp.stack(new_v), jnp.stack(new_s)


def setup_inputs(seed: int = 0) -> dict:
    key = jax.random.key(seed)
    ks = jax.random.split(key, 22)
    f32 = jnp.float32

    def nrm(k, shape, scale):
        return jax.random.normal(k, shape, f32) * scale

    def gain(k, shape):
        return 1.0 + 0.01 * jax.random.normal(k, shape, f32)

    return {
        'x_prompt': nrm(ks[0], (BATCH, SEQ, D_MODEL), 1.0),
        'x_sample': nrm(ks[1], (DEC_BATCH, DEC_SEQ, D_MODEL), 1.0),
        'cache_da_k': nrm(ks[2], (DEPTH, DEC_BATCH, PAST_LEN, DA_HEADS, 2, DA_HD), 1.0),
        'cache_da_v': nrm(ks[3], (DEPTH, DEC_BATCH, PAST_LEN, DA_HEADS, DA_VD), 1.0),
        'state_gla': nrm(ks[4], (DEPTH, DEC_BATCH, GLA_HEADS, GLA_HK, GLA_HV), 1.0),
        'norm1_g': gain(ks[5], (DEPTH, D_MODEL)),
        'w_in': nrm(ks[6], (DEPTH, D_MODEL, IN_WIDTH), D_MODEL ** -0.5),
        'da_lambda_q1': nrm(ks[7], (DEPTH, DA_HD), 0.1),
        'da_lambda_k1': nrm(ks[8], (DEPTH, DA_HD), 0.1),
        'da_lambda_q2': nrm(ks[9], (DEPTH, DA_HD), 0.1),
        'da_lambda_k2': nrm(ks[10], (DEPTH, DA_HD), 0.1),
        'da_subln_g': gain(ks[11], (DEPTH, DA_VD)),
        'gla_w_alpha': nrm(ks[12], (DEPTH, GLA_GATE_RANK, GLA_K), GLA_GATE_RANK ** -0.5),
        'gla_b_alpha': nrm(ks[13], (DEPTH, GLA_K), 0.1),
        'gla_norm_g': gain(ks[14], (DEPTH, GLA_HV)),
        'w_out': nrm(ks[15], (DEPTH, D_MODEL, D_MODEL), D_MODEL ** -0.5),
        'norm2_g': gain(ks[16], (DEPTH, D_MODEL)),
        'peer_wq': nrm(ks[17], (DEPTH, D_MODEL, PEER_HEADS * 2 * PK_DIM), D_MODEL ** -0.5),
        'peer_subkeys': nrm(ks[18], (DEPTH, PEER_HEADS, 2, N_KEYS, PK_DIM), PK_DIM ** -0.5),
        'peer_u': nrm(ks[19], (DEPTH, N_EXPERTS, D_MODEL), D_MODEL ** -0.5),
        'peer_v': nrm(ks[20], (DEPTH, N_EXPERTS, D_MODEL), PEER_HEADS ** -0.5),
        'final_norm_g': gain(ks[21], (D_MODEL,)),
    }


def reference(x_prompt, x_sample, cache_da_k, cache_da_v, state_gla, norm1_g, w_in,
              da_lambda_q1, da_lambda_k1, da_lambda_q2, da_lambda_k2, da_subln_g,
              gla_w_alpha, gla_b_alpha, gla_norm_g, w_out, norm2_g, peer_wq, peer_subkeys,
              peer_u, peer_v, final_norm_g):
    weights = (norm1_g, w_in, da_lambda_q1, da_lambda_k1, da_lambda_q2, da_lambda_k2,
               da_subln_g, gla_w_alpha, gla_b_alpha, gla_norm_g, w_out, norm2_g, peer_wq,
               peer_subkeys, peer_u, peer_v, final_norm_g)
    pos_p = jnp.arange(x_prompt.shape[1], dtype=jnp.float32)
    pos_s = cache_da_k.shape[2] + jnp.arange(x_sample.shape[1], dtype=jnp.float32)
    y_p, k_p, v_p, s_p = trunk(x_prompt, pos_p, None, None, None, CHUNK, *weights)
    y_s, k_s, v_s, s_s = trunk(x_sample, pos_s, cache_da_k, cache_da_v, state_gla,
                               x_sample.shape[1], *weights)
    return (y_p, y_s, k_p, v_p, s_p, k_s, v_s, s_s)
```

```python
import functools
import math

import jax
import jax.numpy as jnp
from jax import lax
from jax.experimental import pallas as pl
from jax.experimental.pallas import tpu as pltpu

F32 = jnp.float32
BF16 = jnp.bfloat16

D_MODEL = 1024
CHUNK = 64
NORM_EPS = 1e-6

DA_HEADS = 8
DA_HD = 64
DA_VD = 128
ROT_DIM = DA_HD // 4
ROPE_THETA = 500000.0

GLA_HEADS = 4
GLA_K = 512
GLA_V = 1024
GLA_HK = 128
GLA_HV = 256
GLA_GATE_RANK = 16
GLA_GATE_TAU = 16.0

PEER_HEADS = 8
N_KEYS = 128
PK_DIM = 128
PEER_TOPK = 16

LANES = 128
VMEM_LIMIT = 56 * 1024 * 1024

NEG_INF = float("-inf")


def _params(*sem):
    return pltpu.CompilerParams(dimension_semantics=sem, vmem_limit_bytes=VMEM_LIMIT)


def _rms(xf, g):
    ms = jnp.mean(xf * xf, axis=-1, keepdims=True)
    return xf * lax.rsqrt(ms + NORM_EPS) * g


def _dot(a, b):
    return jnp.dot(a, b, preferred_element_type=F32)


def _dot_nt(a, b):
    return lax.dot_general(a, b, (((1,), (1,)), ((), ())), preferred_element_type=F32)


def _dot_tn(a, b):
    return lax.dot_general(a, b, (((0,), (0,)), ((), ())), preferred_element_type=F32)


def _da_proj_kernel(x_ref, g_ref, w_ref, wga_ref, cos_ref, sa_ref, sb_ref,
                    q_ref, k32_ref, kbf_ref, v32_ref, vbf_ref, siga_ref):
    h = _rms(x_ref[...], g_ref[...]).astype(BF16)
    cos_t = cos_ref[...]
    sin_a = sa_ref[...]
    sin_b = sb_ref[...]

    def rope(blk):
        return (blk * cos_t + pltpu.roll(blk, LANES - ROT_DIM // 2, 1) * sin_a
                + pltpu.roll(blk, ROT_DIM // 2, 1) * sin_b)

    qf = _dot(h, w_ref[:, 0:D_MODEL])
    for c in range(D_MODEL // LANES):
        sl = slice(c * LANES, (c + 1) * LANES)
        q_ref[:, sl] = (rope(qf[:, sl]) * (DA_HD ** -0.5)).astype(BF16)
    kf = _dot(h, w_ref[:, D_MODEL:2 * D_MODEL])
    for c in range(D_MODEL // LANES):
        sl = slice(c * LANES, (c + 1) * LANES)
        kr = rope(kf[:, sl])
        k32_ref[:, sl] = kr
        kbf_ref[:, sl] = kr.astype(BF16)
    vf = _dot(h, w_ref[:, 2 * D_MODEL:3 * D_MODEL])
    v32_ref[...] = vf
    vbf_ref[...] = vf.astype(BF16)
    siga_ref[...] = jax.nn.sigmoid(_dot(h, wga_ref[...]))


def _da_proj(x2d, g1, w_qkv, w_ga, cos_t, sin_a, sin_b, tm):
    T = x2d.shape[0]
    n_pos_blocks = cos_t.shape[0] // tm
    row = lambda i: (i, 0)
    fixed = lambda i: (0, 0)
    pos = lambda i: (i % n_pos_blocks, 0)
    full = pl.BlockSpec((tm, D_MODEL), row)
    tab = pl.BlockSpec((tm, LANES), pos)
    return pl.pallas_call(
        _da_proj_kernel,
        grid=(T // tm,),
        in_specs=[full, pl.BlockSpec((1, D_MODEL), fixed),
                  pl.BlockSpec((D_MODEL, 3 * D_MODEL), fixed),
                  pl.BlockSpec((D_MODEL, D_MODEL), fixed), tab, tab, tab],
        out_specs=[full] * 6,
        out_shape=[jax.ShapeDtypeStruct((T, D_MODEL), BF16),
                   jax.ShapeDtypeStruct((T, D_MODEL), F32),
                   jax.ShapeDtypeStruct((T, D_MODEL), BF16),
                   jax.ShapeDtypeStruct((T, D_MODEL), F32),
                   jax.ShapeDtypeStruct((T, D_MODEL), BF16),
                   jax.ShapeDtypeStruct((T, D_MODEL), F32)],
        compiler_params=_params("parallel"),
        name="da_proj",
    )(x2d, g1, w_qkv, w_ga, cos_t, sin_a, sin_b)


def _lambda_of(lam_ref, lam_init):
    a = lam_ref[...]
    d1 = jnp.sum(a[0:1, :] * a[1:2, :], axis=-1, keepdims=True)
    d2 = jnp.sum(a[2:3, :] * a[3:4, :], axis=-1, keepdims=True)
    return jnp.exp(d1) - jnp.exp(d2) + lam_init


def _stack_maps(q):
    lane = lax.broadcasted_iota(jnp.int32, q.shape, 1)
    zero = jnp.zeros_like(q)
    return jnp.concatenate([jnp.where(lane < DA_HD, q, zero),
                            jnp.where(lane >= DA_HD, q, zero)], axis=0)


def _softmax_step(qq, kb, vb, carry, mask):
    m, l, acc = carry
    s = _dot_nt(qq, kb)
    if mask is not None:
        s = jnp.where(mask, s, NEG_INF)
    m_new = jnp.maximum(m, jnp.max(s, axis=1, keepdims=True))
    alpha = jnp.exp(m - m_new)
    p = jnp.exp(s - m_new)
    l = alpha * l + jnp.sum(p, axis=1, keepdims=True)
    acc = alpha * acc + _dot(p.astype(BF16), vb)
    return m_new, l, acc


def _da_finish(carry, tq, lam, lam_init, subg, siga):
    _, l, acc = carry
    o = acc[:tq] / l[:tq] - lam * (acc[tq:] / l[tq:])
    return _rms(o, subg) * (1.0 - lam_init) * siga


def _attn_prompt_kernel(lam_ref, q_ref, k_ref, v_ref, siga_ref, subg_ref, o_ref, *, tq, lam_init):
    i = pl.program_id(2)
    qq = _stack_maps(q_ref[...])
    init = (jnp.full((2 * tq, 1), NEG_INF, F32), jnp.zeros((2 * tq, 1), F32),
            jnp.zeros((2 * tq, DA_VD), F32))

    def body(j, carry):
        off = pl.multiple_of(j * tq, tq)
        return _softmax_step(qq, k_ref[pl.ds(off, tq), :], v_ref[pl.ds(off, tq), :], carry, None)

    carry = lax.fori_loop(0, i, body, init)
    off = pl.multiple_of(i * tq, tq)
    r = lax.broadcasted_iota(jnp.int32, (2 * tq, tq), 0)
    c = lax.broadcasted_iota(jnp.int32, (2 * tq, tq), 1)
    r = jnp.where(r >= tq, r - tq, r)
    mask = (c // CHUNK) <= (r // CHUNK)
    carry = _softmax_step(qq, k_ref[pl.ds(off, tq), :], v_ref[pl.ds(off, tq), :], carry, mask)
    o_ref[...] = _da_finish(carry, tq, _lambda_of(lam_ref, lam_init), lam_init,
                            subg_ref[...], siga_ref[...])


def _attn_prompt(lam4, q_bf, k_bf, v_bf, siga, subg, B, L, tq, lam_init):
    nq = L // tq
    qmap = lambda b, h, i: (b * nq + i, h)
    kvmap = lambda b, h, i: (b, h)
    fixed = lambda b, h, i: (0, 0)
    return pl.pallas_call(
        functools.partial(_attn_prompt_kernel, tq=tq, lam_init=lam_init),
        grid=(B, DA_HEADS, nq),
        in_specs=[pl.BlockSpec((4, DA_HD), fixed),
                  pl.BlockSpec((tq, LANES), qmap),
                  pl.BlockSpec((L, LANES), kvmap),
                  pl.BlockSpec((L, LANES), kvmap),
                  pl.BlockSpec((tq, LANES), qmap),
                  pl.BlockSpec((1, DA_VD), fixed)],
        out_specs=pl.BlockSpec((tq, LANES), qmap),
        out_shape=jax.ShapeDtypeStruct((B * L, D_MODEL), F32),
        compiler_params=_params("parallel", "parallel", "arbitrary"),
        name="attn_prompt",
    )(lam4, q_bf, k_bf, v_bf, siga, subg)


def _attn_sample_kernel(lam_ref, q_ref, pk_ref, pv_ref, k_ref, v_ref, siga_ref, subg_ref, o_ref,
                        *, tq, lam_init):
    qq = _stack_maps(q_ref[...])
    init = (jnp.full((2 * tq, 1), NEG_INF, F32), jnp.zeros((2 * tq, 1), F32),
            jnp.zeros((2 * tq, DA_VD), F32))
    carry = _softmax_step(qq, pk_ref[...].astype(BF16), pv_ref[...].astype(BF16), init, None)
    carry = _softmax_step(qq, k_ref[...], v_ref[...], carry, None)
    o_ref[...] = _da_finish(carry, tq, _lambda_of(lam_ref, lam_init), lam_init,
                            subg_ref[...], siga_ref[...])


def _attn_sample(lam4, q_bf, past_k, past_v, k_bf, v_bf, siga, subg, B, L, lam_init):
    P = past_k.shape[1]
    qmap = lambda b, h: (b, h)
    pmap = lambda b, h: (b, 0, h)
    fixed = lambda b, h: (0, 0)
    return pl.pallas_call(
        functools.partial(_attn_sample_kernel, tq=L, lam_init=lam_init),
        grid=(B, DA_HEADS),
        in_specs=[pl.BlockSpec((4, DA_HD), fixed),
                  pl.BlockSpec((L, LANES), qmap),
                  pl.BlockSpec((None, P, LANES), pmap),
                  pl.BlockSpec((None, P, LANES), pmap),
                  pl.BlockSpec((L, LANES), qmap),
                  pl.BlockSpec((L, LANES), qmap),
                  pl.BlockSpec((L, LANES), qmap),
                  pl.BlockSpec((1, DA_VD), fixed)],
        out_specs=pl.BlockSpec((L, LANES), qmap),
        out_shape=jax.ShapeDtypeStruct((B * L, D_MODEL), F32),
        compiler_params=_params("parallel", "parallel"),
        name="attn_sample",
    )(lam4, q_bf, past_k, past_v, k_bf, v_bf, siga, subg)


def _gla_kernel(x_ref, g1_ref, wg_ref, wlr_ref, wa_ref, ba_ref, glag_ref, s0_ref, oda_ref,
                merged_ref, sfin_ref, st_scr, *, rows, chunk):
    t = pl.program_id(1)

    @pl.when(t == 0)
    def _():
        for hh in range(GLA_HEADS):
            st_scr[hh] = s0_ref[hh].T

    h = _rms(x_ref[...], g1_ref[...]).astype(BF16)
    proj = _dot(h, wg_ref[...])
    gq = proj[:, 0:GLA_K]
    gk = proj[:, GLA_K:2 * GLA_K]
    gv = proj[:, 2 * GLA_K:2 * GLA_K + GLA_V]
    gr = proj[:, 2 * GLA_K + GLA_V:2 * GLA_K + 2 * GLA_V]
    gb = proj[:, 2 * GLA_K + 2 * GLA_V:]
    g_lr = _dot(h, wlr_ref[...])
    z = _dot(g_lr.astype(BF16), wa_ref[...]) + ba_ref[...]
    log_a = (jnp.minimum(z, 0.0) - jnp.log1p(jnp.exp(-jnp.abs(z)))) / GLA_GATE_TAU

    ri = lax.broadcasted_iota(jnp.int32, (chunk, chunk), 0)
    ci = lax.broadcasted_iota(jnp.int32, (chunk, chunk), 1)
    causal = ci <= ri
    tri = causal.astype(F32)

    outs = []
    for c in range(rows // chunk):
        rs = slice(c * chunk, (c + 1) * chunk)
        b = jnp.dot(tri, log_a[rs], precision=lax.Precision.HIGHEST, preferred_element_type=F32)
        b_last = b[chunk - 1:chunk, :]
        e_b = jnp.exp(b)
        e_nb = jnp.exp(-b)
        e_kl = jnp.exp(b_last - b)
        e_last = jnp.exp(b_last)
        heads = []
        for hh in range(GLA_HEADS):
            ks = slice(hh * GLA_HK, (hh + 1) * GLA_HK)
            vs = slice(hh * GLA_HV, (hh + 1) * GLA_HV)
            qd = (gq[rs, ks] * (GLA_HK ** -0.5) * e_b[:, ks]).astype(BF16)
            kd = (gk[rs, ks] * e_nb[:, ks]).astype(BF16)
            kl = (gk[rs, ks] * e_kl[:, ks]).astype(BF16)
            vv = gv[rs, vs].astype(BF16)
            att = jnp.where(causal, _dot_nt(qd, kd), 0.0)
            st = st_scr[hh]
            heads.append(_dot(att.astype(BF16), vv) + _dot_nt(qd, st.astype(BF16)))
            st_scr[hh] = st * e_last[:, ks] + _dot_tn(vv, kl)
        outs.append(jnp.concatenate(heads, axis=1))
    o = jnp.concatenate(outs, axis=0) if len(outs) > 1 else outs[0]

    normed = []
    for hh in range(GLA_HEADS):
        vs = slice(hh * GLA_HV, (hh + 1) * GLA_HV)
        normed.append(_rms(o[:, vs], glag_ref[...]))
    o = jnp.concatenate(normed, axis=1) * (gr * jax.nn.sigmoid(gr))
    merged_ref[...] = (oda_ref[...] + jax.nn.sigmoid(gb) * o).astype(BF16)

    @pl.when(t == pl.num_programs(1) - 1)
    def _():
        for hh in range(GLA_HEADS):
            sfin_ref[hh] = st_scr[hh].T


def _gla(x2d, g1, w_g, w_lr, w_a, b_a, gla_g, s0, oda, B, L, rows, chunk):
    nt = L // rows
    rmap = lambda b, t: (b * nt + t, 0)
    fixed = lambda b, t: (0, 0)
    smap = lambda b, t: (b, 0, 0, 0)
    return pl.pallas_call(
        functools.partial(_gla_kernel, rows=rows, chunk=chunk),
        grid=(B, nt),
        in_specs=[pl.BlockSpec((rows, D_MODEL), rmap),
                  pl.BlockSpec((1, D_MODEL), fixed),
                  pl.BlockSpec(w_g.shape, fixed),
                  pl.BlockSpec(w_lr.shape, fixed),
                  pl.BlockSpec(w_a.shape, fixed),
                  pl.BlockSpec((1, GLA_K), fixed),
                  pl.BlockSpec((1, GLA_HV), fixed),
                  pl.BlockSpec((None, GLA_HEADS, GLA_HK, GLA_HV), smap),
                  pl.BlockSpec((rows, D_MODEL), rmap)],
        out_specs=[pl.BlockSpec((rows, D_MODEL), rmap),
                   pl.BlockSpec((None, GLA_HEADS, GLA_HK, GLA_HV), smap)],
        out_shape=[jax.ShapeDtypeStruct((B * L, D_MODEL), BF16),
                   jax.ShapeDtypeStruct((B, GLA_HEADS, GLA_HK, GLA_HV), F32)],
        scratch_shapes=[pltpu.VMEM((GLA_HEADS, GLA_HV, GLA_HK), F32)],
        compiler_params=_params("parallel", "arbitrary"),
        name="gla",
    )(x2d, g1, w_g, w_lr, w_a, b_a, gla_g, s0, oda)


def _peer_query_kernel(x_ref, m_ref, wo_ref, g2_ref, wq_ref, sk_ref, x1_ref, h2_ref, st_ref):
    x1 = x_ref[...] + _dot(m_ref[...], wo_ref[...])
    x1_ref[...] = x1
    h2 = _rms(x1, g2_ref[...]).astype(BF16)
    h2_ref[...] = h2
    q = _dot(h2, wq_ref[...]).astype(BF16)
    for j in range(2 * PEER_HEADS):
        st_ref[j] = _dot_nt(sk_ref[j], q[:, j * PK_DIM:(j + 1) * PK_DIM])


def _peer_query(x2d, merged, w_out, g2, wq, subkeys, tm):
    T = x2d.shape[0]
    row = lambda i: (i, 0)
    fixed = lambda i: (0, 0)
    nj = 2 * PEER_HEADS
    return pl.pallas_call(
        _peer_query_kernel,
        grid=(T // tm,),
        in_specs=[pl.BlockSpec((tm, D_MODEL), row),
                  pl.BlockSpec((tm, D_MODEL), row),
                  pl.BlockSpec((D_MODEL, D_MODEL), fixed),
                  pl.BlockSpec((1, D_MODEL), fixed),
                  pl.BlockSpec((D_MODEL, nj * PK_DIM), fixed),
                  pl.BlockSpec((nj, N_KEYS, PK_DIM), lambda i: (0, 0, 0))],
        out_specs=[pl.BlockSpec((tm, D_MODEL), row),
                   pl.BlockSpec((tm, D_MODEL), row),
                   pl.BlockSpec((nj, N_KEYS, tm), lambda i: (0, 0, i))],
        out_shape=[jax.ShapeDtypeStruct((T, D_MODEL), F32),
                   jax.ShapeDtypeStruct((T, D_MODEL), BF16),
                   jax.ShapeDtypeStruct((nj, N_KEYS, T), F32)],
        compiler_params=_params("parallel"),
        name="peer_query",
    )(x2d, merged, w_out, g2, wq, subkeys)


def _top_sorted(s, k):
    work = s
    rows = []
    for _ in range(k):
        m = jnp.max(work, axis=0, keepdims=True)
        rows.append(m)
        work = jnp.where(work == m, NEG_INF, work)
    return jnp.concatenate(rows, axis=0)


def _peer_route_kernel(st_ref, e1_ref, e2_ref, th_ref):
    s1 = st_ref[0]
    s2 = st_ref[1]
    a = _top_sorted(s1, PEER_TOPK)
    b = _top_sorted(s2, PEER_TOPK)
    half = PEER_TOPK // 2
    cands = [a[0:half] + b[0:1], a[half:] + b[0:1], a[0:1] + b[half:]]
    for c in range(1, half):
        cands.append(a[0:half] + b[c:c + 1])
    cand = jnp.concatenate(cands, axis=0)
    work = cand
    for _ in range(PEER_TOPK):
        theta = jnp.max(work, axis=0, keepdims=True)
        work = jnp.where(work == theta, NEG_INF, work)
    top = a[0:1] + b[0:1]
    zsum = jnp.sum(jnp.where(cand >= theta, jnp.exp(cand - top), 0.0), axis=0, keepdims=True)
    e1_ref[0] = jnp.exp(s1 - a[0:1])
    e2_ref[0] = jnp.exp(s2 - b[0:1]) / zsum
    th_ref[0] = theta


def _peer_route(st, tt):
    T = st.shape[-1]
    return pl.pallas_call(
        _peer_route_kernel,
        grid=(PEER_HEADS, T // tt),
        in_specs=[pl.BlockSpec((None, 2, N_KEYS, tt), lambda h, i: (h, 0, 0, i))],
        out_specs=[pl.BlockSpec((1, N_KEYS, tt), lambda h, i: (h, 0, i)),
                   pl.BlockSpec((1, N_KEYS, tt), lambda h, i: (h, 0, i)),
                   pl.BlockSpec((1, 1, tt), lambda h, i: (h, 0, i))],
        out_shape=[jax.ShapeDtypeStruct((PEER_HEADS, N_KEYS, T), F32),
                   jax.ShapeDtypeStruct((PEER_HEADS, N_KEYS, T), F32),
                   jax.ShapeDtypeStruct((PEER_HEADS, 1, T), F32)],
        compiler_params=_params("parallel", "parallel"),
        name="peer_route",
    )(st)


def _peer_dense_kernel(h2_ref, x1_ref, s1_ref, s2_ref, e1_ref, e2_ref, th_ref, u_ref, vt_ref, gf_ref,
                       y_ref, a_scr, p_scr, acc_scr, *, tt, ni):
    e = pl.program_id(1)

    @pl.when(e == 0)
    def _():
        acc_scr[...] = jnp.zeros_like(acc_scr)

    a_scr[...] = _dot_nt(u_ref[...], h2_ref[...])

    def per_strip(ls, carry):
        cols = pl.ds(pl.multiple_of(ls * LANES, LANES), LANES)
        for il in range(ni):
            rows = slice(il * N_KEYS, (il + 1) * N_KEYS)
            w = jnp.zeros((N_KEYS, LANES), F32)
            for hh in range(PEER_HEADS):
                s1row = s1_ref[hh, il:il + 1, cols]
                e1row = e1_ref[hh, il:il + 1, cols]
                sel = (s1row + s2_ref[hh, :, cols]) >= th_ref[hh, :, cols]
                w = w + jnp.where(sel, e2_ref[hh, :, cols], 0.0) * e1row
            a = a_scr[rows, cols]
            act = 0.5 * a * (1.0 + lax.erf(a * (2.0 ** -0.5)))
            p_scr[rows, cols] = (w * act).astype(BF16)
        return carry

    lax.fori_loop(0, tt // LANES, per_strip, 0)
    acc_scr[...] += _dot(vt_ref[...], p_scr[...])

    @pl.when(e == pl.num_programs(1) - 1)
    def _():
        y_ref[...] = _rms(x1_ref[...] + acc_scr[...].T, gf_ref[...])


def _peer_dense(h2, x1, st, e1, e2, th, u_bf, vt_bf, gf, tt, ni):
    T = h2.shape[0]
    ne = N_KEYS // ni
    et = ni * N_KEYS
    tok = lambda t, e: (t, 0)
    return pl.pallas_call(
        functools.partial(_peer_dense_kernel, tt=tt, ni=ni),
        grid=(T // tt, ne),
        in_specs=[pl.BlockSpec((tt, D_MODEL), tok),
                  pl.BlockSpec((tt, D_MODEL), tok),
                  pl.BlockSpec((PEER_HEADS, None, ni, tt), lambda t, e: (0, 0, e, t)),
                  pl.BlockSpec((PEER_HEADS, None, N_KEYS, tt), lambda t, e: (0, 1, 0, t)),
                  pl.BlockSpec((PEER_HEADS, ni, tt), lambda t, e: (0, e, t)),
                  pl.BlockSpec((PEER_HEADS, N_KEYS, tt), lambda t, e: (0, 0, t)),
                  pl.BlockSpec((PEER_HEADS, 1, tt), lambda t, e: (0, 0, t)),
                  pl.BlockSpec((et, D_MODEL), lambda t, e: (e, 0)),
                  pl.BlockSpec((D_MODEL, et), lambda t, e: (0, e)),
                  pl.BlockSpec((1, D_MODEL), lambda t, e: (0, 0))],
        out_specs=pl.BlockSpec((tt, D_MODEL), tok),
        out_shape=jax.ShapeDtypeStruct((T, D_MODEL), F32),
        scratch_shapes=[pltpu.VMEM((et, tt), F32),
                        pltpu.VMEM((et, tt), BF16),
                        pltpu.VMEM((D_MODEL, tt), F32)],
        compiler_params=_params("parallel", "arbitrary"),
        name="peer_dense",
    )(h2, x1, st, st, e1, e2, th, u_bf, vt_bf, gf)


def _rope_tables(pos):
    half = ROT_DIM // 2
    inv = jnp.power(ROPE_THETA, -jnp.arange(half, dtype=F32) * 2.0 / ROT_DIM)
    ang = pos.astype(F32)[:, None] * inv[None, :]
    cos, sin = jnp.cos(ang), jnp.sin(ang)
    n = pos.shape[0]
    pad = jnp.zeros((n, DA_HD - ROT_DIM), F32)
    zero = jnp.zeros((n, half), F32)
    cos_t = jnp.concatenate([cos, cos, pad + 1.0], axis=1)
    sin_a = jnp.concatenate([-sin, zero, pad], axis=1)
    sin_b = jnp.concatenate([zero, sin, pad], axis=1)
    return tuple(jnp.tile(t, (1, LANES // DA_HD)) for t in (cos_t, sin_a, sin_b))


def _trunk(x, pos, past_k, past_v, s0, chunk, wts, layer, tm, tq, gla_rows, tt, ni):
    B, L, _ = x.shape
    T = B * L
    lam_init = 0.8 - 0.6 * math.exp(-0.3 * layer)
    x2d = x.reshape(T, D_MODEL)
    cos_t, sin_a, sin_b = _rope_tables(pos)
    if L < tm:
        cos_t, sin_a, sin_b = (jnp.tile(t, (tm // L, 1)) for t in (cos_t, sin_a, sin_b))

    q_bf, k32, k_bf, v32, v_bf, siga = _da_proj(
        x2d, wts["g1"], wts["w_qkv"], wts["w_ga"], cos_t, sin_a, sin_b, tm)
    if past_k is None:
        oda = _attn_prompt(wts["lam4"], q_bf, k_bf, v_bf, siga, wts["subg"], B, L, tq, lam_init)
    else:
        oda = _attn_sample(wts["lam4"], q_bf, past_k, past_v, k_bf, v_bf, siga, wts["subg"],
                           B, L, lam_init)
    merged, s_fin = _gla(x2d, wts["g1"], wts["w_g"], wts["w_lr"], wts["w_a"], wts["b_a"],
                         wts["gla_g"], s0, oda, B, L, gla_rows, chunk)
    x1, h2, st = _peer_query(x2d, merged, wts["w_out"], wts["g2"], wts["wq"], wts["subkeys"], tm)
    st = st.reshape(PEER_HEADS, 2, N_KEYS, T)
    e1, e2, th = _peer_route(st, tt)
    y = _peer_dense(h2, x1, st, e1, e2, th, wts["u"], wts["vt"], wts["gf"], tt, ni)
    return (y.reshape(B, L, D_MODEL), k32.reshape(1, B, L, DA_HEADS, 2, DA_HD),
            v32.reshape(1, B, L, DA_HEADS, DA_VD), s_fin[None])


def kernel(x_prompt, x_sample, cache_da_k, cache_da_v, state_gla, norm1_g, w_in, da_lambda_q1,
           da_lambda_k1, da_lambda_q2, da_lambda_k2, da_subln_g, gla_w_alpha, gla_b_alpha,
           gla_norm_g, w_out, norm2_g, peer_wq, peer_subkeys, peer_u, peer_v, final_norm_g):
    layer = 0
    B, L, _ = x_prompt.shape
    SB, SL, _ = x_sample.shape
    P = cache_da_k.shape[2]

    o_dq, o_gq = 0, 3 * D_MODEL
    o_gr = o_gq + 2 * GLA_K + GLA_V
    o_lr = o_gr + GLA_V
    o_ga = o_lr + GLA_GATE_RANK
    o_gb = o_ga + D_MODEL
    w = w_in[layer]
    pad_rank = LANES - GLA_GATE_RANK
    wts = {
        "g1": norm1_g[layer][None, :],
        "w_qkv": w[:, o_dq:o_gq].astype(BF16),
        "w_ga": w[:, o_ga:o_gb].astype(BF16),
        "w_g": jnp.concatenate([w[:, o_gq:o_lr], w[:, o_gb:]], axis=1).astype(BF16),
        "w_lr": jnp.pad(w[:, o_lr:o_ga], ((0, 0), (0, pad_rank))).astype(BF16),
        "w_a": jnp.pad(gla_w_alpha[layer], ((0, pad_rank), (0, 0))).astype(BF16),
        "b_a": gla_b_alpha[layer][None, :],
        "gla_g": gla_norm_g[layer][None, :],
        "lam4": jnp.stack([da_lambda_q1[layer], da_lambda_k1[layer],
                           da_lambda_q2[layer], da_lambda_k2[layer]]),
        "subg": da_subln_g[layer][None, :],
        "w_out": w_out[layer].astype(BF16),
        "g2": norm2_g[layer][None, :],
        "wq": peer_wq[layer].astype(BF16),
        "subkeys": peer_subkeys[layer].reshape(2 * PEER_HEADS, N_KEYS, PK_DIM).astype(BF16),
        "u": peer_u[layer].astype(BF16),
        "vt": peer_v[layer].astype(BF16).T,
        "gf": final_norm_g[None, :],
    }

    pos_p = jnp.arange(L, dtype=F32)
    pos_s = P + jnp.arange(SL, dtype=F32)
    zeros_state = jnp.zeros((B, GLA_HEADS, GLA_HK, GLA_HV), F32)
    tm_p = min(256, B * L)
    y_p, k_p, v_p, s_p = _trunk(x_prompt, pos_p, None, None, zeros_state, CHUNK, wts, layer,
                                tm=tm_p, tq=min(256, L), gla_rows=min(256, L),
                                tt=min(512, B * L), ni=8)
    past_k = cache_da_k[layer].reshape(SB, P, D_MODEL)
    past_v = cache_da_v[layer].reshape(SB, P, D_MODEL)
    y_s, k_s, v_s, s_s = _trunk(x_sample, pos_s, past_k, past_v, state_gla[layer], SL, wts, layer,
                                tm=SB * SL, tq=SL, gla_rows=SL, tt=SB * SL, ni=8)
    return (y_p, y_s, k_p, v_p, s_p, k_s, v_s, s_s)
```

```python
import functools
import math

import jax
import jax.numpy as jnp
from jax import lax
from jax.experimental import pallas as pl
from jax.experimental.pallas import tpu as pltpu

F32 = jnp.float32
BF16 = jnp.bfloat16

D_MODEL = 1024
CHUNK = 64
NORM_EPS = 1e-6

DA_HEADS = 8
DA_HD = 64
DA_VD = 128
ROT_DIM = DA_HD // 4
ROPE_THETA = 500000.0
Q_SCALE = DA_HD ** -0.5 * math.log2(math.e)

GLA_HEADS = 4
GLA_K = 512
GLA_V = 1024
GLA_HK = 128
GLA_HV = 256
GLA_GATE_RANK = 16
GLA_GATE_TAU = 16.0

PEER_HEADS = 8
N_KEYS = 128
PK_DIM = 128
PEER_TOPK = 16

LANES = 128
VMEM_LIMIT = 56 * 1024 * 1024

NEG_INF = float("-inf")


def _params(*sem):
    return pltpu.CompilerParams(dimension_semantics=sem, vmem_limit_bytes=VMEM_LIMIT)


def _rms(xf, g):
    ms = jnp.mean(xf * xf, axis=-1, keepdims=True)
    return xf * lax.rsqrt(ms + NORM_EPS) * g


def _dot(a, b):
    return jnp.dot(a, b, preferred_element_type=F32)


def _dot_nt(a, b):
    return lax.dot_general(a, b, (((1,), (1,)), ((), ())), preferred_element_type=F32)


def _dot_tn(a, b):
    return lax.dot_general(a, b, (((0,), (0,)), ((), ())), preferred_element_type=F32)


def _da_proj_kernel(x_ref, g_ref, w_ref, wga_ref, cos_ref, sa_ref, sb_ref,
                    q_ref, k32_ref, kbf_ref, v32_ref, vbf_ref, siga_ref, *, transpose_v):
    h = _rms(x_ref[...], g_ref[...]).astype(BF16)
    cos_t = cos_ref[...]
    sin_a = sa_ref[...]
    sin_b = sb_ref[...]

    def rope(blk):
        return (blk * cos_t + pltpu.roll(blk, LANES - ROT_DIM // 2, 1) * sin_a
                + pltpu.roll(blk, ROT_DIM // 2, 1) * sin_b)

    qf = _dot(h, w_ref[:, 0:D_MODEL])
    for c in range(D_MODEL // LANES):
        sl = slice(c * LANES, (c + 1) * LANES)
        q_ref[:, sl] = (rope(qf[:, sl]) * Q_SCALE).astype(BF16)
    kf = _dot(h, w_ref[:, D_MODEL:2 * D_MODEL])
    for c in range(D_MODEL // LANES):
        sl = slice(c * LANES, (c + 1) * LANES)
        kr = rope(kf[:, sl])
        k32_ref[:, sl] = kr
        kbf_ref[:, sl] = kr.astype(BF16)
    vf = _dot(h, w_ref[:, 2 * D_MODEL:3 * D_MODEL])
    v32_ref[...] = vf
    vbf_ref[...] = (vf.T if transpose_v else vf).astype(BF16)
    siga_ref[...] = jax.nn.sigmoid(_dot(h, wga_ref[...]))


def _da_proj(x2d, g1, w_qkv, w_ga, cos_t, sin_a, sin_b, tm, transpose_v):
    T = x2d.shape[0]
    n_pos_blocks = cos_t.shape[0] // tm
    row = lambda i: (i, 0)
    fixed = lambda i: (0, 0)
    pos = lambda i: (i % n_pos_blocks, 0)
    full = pl.BlockSpec((tm, D_MODEL), row)
    tab = pl.BlockSpec((tm, LANES), pos)
    if transpose_v:
        v_spec = pl.BlockSpec((D_MODEL, tm), lambda i: (0, i))
        v_shape = jax.ShapeDtypeStruct((D_MODEL, T), BF16)
    else:
        v_spec, v_shape = full, jax.ShapeDtypeStruct((T, D_MODEL), BF16)
    return pl.pallas_call(
        functools.partial(_da_proj_kernel, transpose_v=transpose_v),
        grid=(T // tm,),
        in_specs=[full, pl.BlockSpec((1, D_MODEL), fixed),
                  pl.BlockSpec((D_MODEL, 3 * D_MODEL), fixed),
                  pl.BlockSpec((D_MODEL, D_MODEL), fixed), tab, tab, tab],
        out_specs=[full, full, full, full, v_spec, full],
        out_shape=[jax.ShapeDtypeStruct((T, D_MODEL), BF16),
                   jax.ShapeDtypeStruct((T, D_MODEL), F32),
                   jax.ShapeDtypeStruct((T, D_MODEL), BF16),
                   jax.ShapeDtypeStruct((T, D_MODEL), F32),
                   v_shape,
                   jax.ShapeDtypeStruct((T, D_MODEL), F32)],
        compiler_params=_params("parallel"),
        name="da_proj",
    )(x2d, g1, w_qkv, w_ga, cos_t, sin_a, sin_b)


def _lambda_of(lam_ref, lam_init):
    a = lam_ref[...]
    d1 = jnp.sum(a[0:1, :] * a[1:2, :], axis=-1, keepdims=True)
    d2 = jnp.sum(a[2:3, :] * a[3:4, :], axis=-1, keepdims=True)
    return jnp.exp(d1) - jnp.exp(d2) + lam_init


def _stack_maps(q):
    lane = lax.broadcasted_iota(jnp.int32, q.shape, 1)
    zero = jnp.zeros_like(q)
    return jnp.concatenate([jnp.where(lane < DA_HD, q, zero),
                            jnp.where(lane >= DA_HD, q, zero)], axis=0)


def _softmax_step(qq, kb, vb, carry, mask):
    m, l, acc = carry
    s = _dot_nt(qq, kb)
    if mask is not None:
        s = jnp.where(mask, s, NEG_INF)
    m_new = jnp.maximum(m, jnp.max(s, axis=1, keepdims=True))
    alpha = jnp.exp2(m - m_new)
    p = jnp.exp2(s - m_new)
    l = alpha * l + jnp.sum(p, axis=1, keepdims=True)
    acc = alpha * acc + _dot(p.astype(BF16), vb)
    return m_new, l, acc


def _da_finish(carry, tq, lam, lam_init, subg, siga):
    _, l, acc = carry
    o = acc[:tq] / l[:tq] - lam * (acc[tq:] / l[tq:])
    return _rms(o, subg) * (1.0 - lam_init) * siga


def _softmax_update(s, m, l):
    m_new = jnp.maximum(m, jnp.max(s, axis=0, keepdims=True))
    alpha = jnp.exp2(m - m_new)
    p = jnp.exp2(s - m_new)
    return m_new, alpha * l + jnp.sum(p, axis=0, keepdims=True), alpha, p.astype(BF16)


def _attn_prompt_kernel(lam_ref, q_ref, k_ref, vt_ref, siga_ref, subg_ref, o_ref,
                        s_scr, p_scr, acc_scr, *, tq, nh, lam_init):
    i = pl.program_id(2)
    hs = [slice(h * LANES, (h + 1) * LANES) for h in range(nh)]
    qq = [_stack_maps(q_ref[:, s]) for s in hs]

    def scores(h, blk):
        return _dot_nt(k_ref[pl.ds(pl.multiple_of(blk * tq, tq), tq), hs[h]], qq[h])

    def values(h, blk):
        return _dot(vt_ref[hs[h], pl.ds(pl.multiple_of(blk * tq, tq), tq)], p_scr[h])

    def advance(h, j, stats, mask):
        m, l, a_prev = stats
        acc_scr[h] = a_prev * acc_scr[h] + values(h, jnp.maximum(j - 1, 0))
        s = s_scr[h]
        if mask is not None:
            s = jnp.where(mask, s, NEG_INF)
        m, l, alpha, p = _softmax_update(s, m, l)
        p_scr[h] = p
        return m, l, alpha

    def trip(j, stats):
        out = []
        for h in range(nh):
            out.append(advance(h, j, stats[h], None))
            s_scr[h] = scores(h, j + 1)
        return tuple(out)

    for h in range(nh):
        s_scr[h] = scores(h, 0)
        p_scr[h] = jnp.zeros((tq, 2 * tq), BF16)
        acc_scr[h] = jnp.zeros((DA_VD, 2 * tq), F32)
    init = (jnp.full((1, 2 * tq), NEG_INF, F32), jnp.zeros((1, 2 * tq), F32),
            jnp.ones((1, 2 * tq), F32))
    stats = lax.fori_loop(0, i, trip, (init,) * nh)

    key = lax.broadcasted_iota(jnp.int32, (tq, 2 * tq), 0)
    qry = lax.broadcasted_iota(jnp.int32, (tq, 2 * tq), 1)
    qry = jnp.where(qry >= tq, qry - tq, qry)
    mask = (key // CHUNK) <= (qry // CHUNK)
    lam = _lambda_of(lam_ref, lam_init)
    for h in range(nh):
        _, l, alpha = advance(h, i, stats[h], mask)
        acc = alpha * acc_scr[h] + values(h, i)
        o_t = acc[:, :tq] / l[:, :tq] - lam * (acc[:, tq:] / l[:, tq:])
        o_ref[:, hs[h]] = _rms(o_t.T, subg_ref[...]) * (1.0 - lam_init) * siga_ref[:, hs[h]]


def _attn_prompt(lam4, q_bf, k_bf, vt_bf, siga, subg, B, L, tq, lam_init, nh=2):
    nq = L // tq
    qmap = lambda b, h, i: (b * nq + i, h)
    fixed = lambda b, h, i: (0, 0)
    return pl.pallas_call(
        functools.partial(_attn_prompt_kernel, tq=tq, nh=nh, lam_init=lam_init),
        grid=(B, DA_HEADS // nh, nq),
        in_specs=[pl.BlockSpec((4, DA_HD), fixed),
                  pl.BlockSpec((tq, nh * LANES), qmap),
                  pl.BlockSpec((L, nh * LANES), lambda b, h, i: (b, h)),
                  pl.BlockSpec((nh * DA_VD, L), lambda b, h, i: (h, b)),
                  pl.BlockSpec((tq, nh * LANES), qmap),
                  pl.BlockSpec((1, DA_VD), fixed)],
        out_specs=pl.BlockSpec((tq, nh * LANES), qmap),
        out_shape=jax.ShapeDtypeStruct((B * L, D_MODEL), F32),
        scratch_shapes=[pltpu.VMEM((nh, tq, 2 * tq), F32),
                        pltpu.VMEM((nh, tq, 2 * tq), BF16),
                        pltpu.VMEM((nh, DA_VD, 2 * tq), F32)],
        compiler_params=_params("parallel", "parallel", "arbitrary"),
        name="attn_prompt",
    )(lam4, q_bf, k_bf, vt_bf, siga, subg)


def _attn_sample_kernel(lam_ref, q_ref, pk_ref, pv_ref, k_ref, v_ref, siga_ref, subg_ref, o_ref,
                        *, tq, lam_init):
    qq = _stack_maps(q_ref[...])
    init = (jnp.full((2 * tq, 1), NEG_INF, F32), jnp.zeros((2 * tq, 1), F32),
            jnp.zeros((2 * tq, DA_VD), F32))
    carry = _softmax_step(qq, pk_ref[...].astype(BF16), pv_ref[...].astype(BF16), init, None)
    carry = _softmax_step(qq, k_ref[...], v_ref[...], carry, None)
    o_ref[...] = _da_finish(carry, tq, _lambda_of(lam_ref, lam_init), lam_init,
                            subg_ref[...], siga_ref[...])


def _attn_sample(lam4, q_bf, past_k, past_v, k_bf, v_bf, siga, subg, B, L, lam_init):
    P = past_k.shape[1]
    qmap = lambda b, h: (b, h)
    pmap = lambda b, h: (b, 0, h)
    fixed = lambda b, h: (0, 0)
    return pl.pallas_call(
        functools.partial(_attn_sample_kernel, tq=L, lam_init=lam_init),
        grid=(B, DA_HEADS),
        in_specs=[pl.BlockSpec((4, DA_HD), fixed),
                  pl.BlockSpec((L, LANES), qmap),
                  pl.BlockSpec((None, P, LANES), pmap),
                  pl.BlockSpec((None, P, LANES), pmap),
                  pl.BlockSpec((L, LANES), qmap),
                  pl.BlockSpec((L, LANES), qmap),
                  pl.BlockSpec((L, LANES), qmap),
                  pl.BlockSpec((1, DA_VD), fixed)],
        out_specs=pl.BlockSpec((L, LANES), qmap),
        out_shape=jax.ShapeDtypeStruct((B * L, D_MODEL), F32),
        compiler_params=_params("parallel", "parallel"),
        name="attn_sample",
    )(lam4, q_bf, past_k, past_v, k_bf, v_bf, siga, subg)


def _gla_kernel(x_ref, g1_ref, wg_ref, wlr_ref, wa_ref, ba_ref, glag_ref, s0_ref, oda_ref,
                merged_ref, sfin_ref, st_scr, *, rows, chunk):
    t = pl.program_id(1)

    @pl.when(t == 0)
    def _():
        for hh in range(GLA_HEADS):
            st_scr[hh] = s0_ref[hh].T

    h = _rms(x_ref[...], g1_ref[...]).astype(BF16)
    proj = _dot(h, wg_ref[...])
    gq = proj[:, 0:GLA_K]
    gk = proj[:, GLA_K:2 * GLA_K]
    gv = proj[:, 2 * GLA_K:2 * GLA_K + GLA_V]
    gr = proj[:, 2 * GLA_K + GLA_V:2 * GLA_K + 2 * GLA_V]
    gb = proj[:, 2 * GLA_K + 2 * GLA_V:]
    g_lr = _dot(h, wlr_ref[...])
    z = _dot(g_lr.astype(BF16), wa_ref[...]) + ba_ref[...]
    log_a = (jnp.minimum(z, 0.0) - jnp.log1p(jnp.exp(-jnp.abs(z)))) / GLA_GATE_TAU

    ri = lax.broadcasted_iota(jnp.int32, (chunk, chunk), 0)
    ci = lax.broadcasted_iota(jnp.int32, (chunk, chunk), 1)
    causal = ci <= ri
    tri = causal.astype(F32)

    outs = []
    for c in range(rows // chunk):
        rs = slice(c * chunk, (c + 1) * chunk)
        b = jnp.dot(tri, log_a[rs], precision=lax.Precision.HIGHEST, preferred_element_type=F32)
        b_last = b[chunk - 1:chunk, :]
        e_b = jnp.exp(b)
        e_nb = jnp.exp(-b)
        e_kl = jnp.exp(b_last - b)
        e_last = jnp.exp(b_last)
        heads = []
        for hh in range(GLA_HEADS):
            ks = slice(hh * GLA_HK, (hh + 1) * GLA_HK)
            vs = slice(hh * GLA_HV, (hh + 1) * GLA_HV)
            qd = (gq[rs, ks] * (GLA_HK ** -0.5) * e_b[:, ks]).astype(BF16)
            kd = (gk[rs, ks] * e_nb[:, ks]).astype(BF16)
            kl = (gk[rs, ks] * e_kl[:, ks]).astype(BF16)
            vv = gv[rs, vs].astype(BF16)
            att = jnp.where(causal, _dot_nt(qd, kd), 0.0)
            st = st_scr[hh]
            heads.append(_dot(att.astype(BF16), vv) + _dot_nt(qd, st.astype(BF16)))
            st_scr[hh] = st * e_last[:, ks] + _dot_tn(vv, kl)
        outs.append(jnp.concatenate(heads, axis=1))
    o = jnp.concatenate(outs, axis=0) if len(outs) > 1 else outs[0]

    normed = []
    for hh in range(GLA_HEADS):
        vs = slice(hh * GLA_HV, (hh + 1) * GLA_HV)
        normed.append(_rms(o[:, vs], glag_ref[...]))
    o = jnp.concatenate(normed, axis=1) * (gr * jax.nn.sigmoid(gr))
    merged_ref[...] = (oda_ref[...] + jax.nn.sigmoid(gb) * o).astype(BF16)

    @pl.when(t == pl.num_programs(1) - 1)
    def _():
        for hh in range(GLA_HEADS):
            sfin_ref[hh] = st_scr[hh].T


def _gla(x2d, g1, w_g, w_lr, w_a, b_a, gla_g, s0, oda, B, L, rows, chunk):
    nt = L // rows
    rmap = lambda b, t: (b * nt + t, 0)
    fixed = lambda b, t: (0, 0)
    smap = lambda b, t: (b, 0, 0, 0)
    return pl.pallas_call(
        functools.partial(_gla_kernel, rows=rows, chunk=chunk),
        grid=(B, nt),
        in_specs=[pl.BlockSpec((rows, D_MODEL), rmap),
                  pl.BlockSpec((1, D_MODEL), fixed),
                  pl.BlockSpec(w_g.shape, fixed),
                  pl.BlockSpec(w_lr.shape, fixed),
                  pl.BlockSpec(w_a.shape, fixed),
                  pl.BlockSpec((1, GLA_K), fixed),
                  pl.BlockSpec((1, GLA_HV), fixed),
                  pl.BlockSpec((None, GLA_HEADS, GLA_HK, GLA_HV), smap),
                  pl.BlockSpec((rows, D_MODEL), rmap)],
        out_specs=[pl.BlockSpec((rows, D_MODEL), rmap),
                   pl.BlockSpec((None, GLA_HEADS, GLA_HK, GLA_HV), smap)],
        out_shape=[jax.ShapeDtypeStruct((B * L, D_MODEL), BF16),
                   jax.ShapeDtypeStruct((B, GLA_HEADS, GLA_HK, GLA_HV), F32)],
        scratch_shapes=[pltpu.VMEM((GLA_HEADS, GLA_HV, GLA_HK), F32)],
        compiler_params=_params("parallel", "arbitrary"),
        name="gla",
    )(x2d, g1, w_g, w_lr, w_a, b_a, gla_g, s0, oda)


def _peer_query_kernel(x_ref, m_ref, wo_ref, g2_ref, wq_ref, sk_ref, x1_ref, h2t_ref, st_ref):
    x1 = x_ref[...] + _dot(m_ref[...], wo_ref[...])
    x1_ref[...] = x1
    h2f = _rms(x1, g2_ref[...])
    h2t_ref[...] = h2f.T.astype(BF16)
    q = _dot(h2f.astype(BF16), wq_ref[...]).astype(BF16)
    for j in range(2 * PEER_HEADS):
        s = _dot_nt(sk_ref[j], q[:, j * PK_DIM:(j + 1) * PK_DIM])
        for strip in range(s.shape[1] // LANES):
            st_ref[j, strip] = s[:, strip * LANES:(strip + 1) * LANES]


def _peer_query(x2d, merged, w_out, g2, wq, subkeys, tm):
    T = x2d.shape[0]
    row = lambda i: (i, 0)
    fixed = lambda i: (0, 0)
    nj = 2 * PEER_HEADS
    return pl.pallas_call(
        _peer_query_kernel,
        grid=(T // tm,),
        in_specs=[pl.BlockSpec((tm, D_MODEL), row),
                  pl.BlockSpec((tm, D_MODEL), row),
                  pl.BlockSpec((D_MODEL, D_MODEL), fixed),
                  pl.BlockSpec((1, D_MODEL), fixed),
                  pl.BlockSpec((D_MODEL, nj * PK_DIM), fixed),
                  pl.BlockSpec((nj, N_KEYS, PK_DIM), lambda i: (0, 0, 0))],
        out_specs=[pl.BlockSpec((tm, D_MODEL), row),
                   pl.BlockSpec((D_MODEL, tm), lambda i: (0, i)),
                   pl.BlockSpec((nj, tm // LANES, N_KEYS, LANES), lambda i: (0, i, 0, 0))],
        out_shape=[jax.ShapeDtypeStruct((T, D_MODEL), F32),
                   jax.ShapeDtypeStruct((D_MODEL, T), BF16),
                   jax.ShapeDtypeStruct((nj, T // LANES, N_KEYS, LANES), F32)],
        compiler_params=_params("parallel"),
        name="peer_query",
    )(x2d, merged, w_out, g2, wq, subkeys)


def _top_sorted(s, k):
    work = s
    rows = []
    for _ in range(k):
        m = jnp.max(work, axis=0, keepdims=True)
        rows.append(m)
        work = jnp.where(work == m, NEG_INF, work)
    return jnp.concatenate(rows, axis=0)


def _peer_route_kernel(st_ref, e1_ref, e2_ref, th_ref):
    for strip in range(st_ref.shape[1]):
        e1, e2, theta = _route_strip(st_ref[0, strip], st_ref[1, strip])
        e1_ref[strip] = e1
        e2_ref[strip] = e2
        th_ref[:, strip * LANES:(strip + 1) * LANES] = theta


def _route_strip(s1, s2):
    a = _top_sorted(s1, PEER_TOPK)
    b = _top_sorted(s2, PEER_TOPK)
    half = PEER_TOPK // 2
    cands = [a[0:half] + b[0:1], a[half:] + b[0:1], a[0:1] + b[half:]]
    for c in range(1, half):
        cands.append(a[0:half] + b[c:c + 1])
    cand = jnp.concatenate(cands, axis=0)
    work = cand
    for _ in range(PEER_TOPK):
        theta = jnp.max(work, axis=0, keepdims=True)
        work = jnp.where(work == theta, NEG_INF, work)
    top = a[0:1] + b[0:1]
    zsum = jnp.sum(jnp.where(cand >= theta, jnp.exp(cand - top), 0.0), axis=0, keepdims=True)
    return jnp.exp(s1 - a[0:1]), jnp.exp(s2 - b[0:1]) / zsum, theta


def _peer_route(st, tt):
    ns_all = st.shape[2]
    ns = tt // LANES
    tile = pl.BlockSpec((None, ns, N_KEYS, LANES), lambda h, i: (h, i, 0, 0))
    return pl.pallas_call(
        _peer_route_kernel,
        grid=(PEER_HEADS, ns_all // ns),
        in_specs=[pl.BlockSpec((None, 2, ns, N_KEYS, LANES), lambda h, i: (h, 0, i, 0, 0))],
        out_specs=[tile, tile, pl.BlockSpec((None, 1, tt), lambda h, i: (h, 0, i))],
        out_shape=[jax.ShapeDtypeStruct((PEER_HEADS, ns_all, N_KEYS, LANES), F32),
                   jax.ShapeDtypeStruct((PEER_HEADS, ns_all, N_KEYS, LANES), F32),
                   jax.ShapeDtypeStruct((PEER_HEADS, 1, ns_all * LANES), F32)],
        compiler_params=_params("parallel", "parallel"),
        name="peer_route",
    )(st)


def _peer_dense_kernel(h2t_ref, x1_ref, s1_ref, s2_ref, e1_ref, e2_ref, th_ref, u_ref, vt_ref, gf_ref,
                       y_ref, a0_scr, a1_scr, p0_scr, p1_scr, acc_scr, *, tt, ni):
    e = pl.program_id(1)
    n_strips = tt // LANES
    a_rows = ni * N_KEYS // n_strips
    d_rows = D_MODEL // n_strips

    @pl.when(e == 0)
    def _():
        for ref in (a0_scr, a1_scr, p0_scr, p1_scr, acc_scr):
            ref[...] = jnp.zeros_like(ref)

    def step(a_new, a_mid, p_old, p_mid):

        def per_strip(ls, carry):
            ar = pl.ds(pl.multiple_of(ls * a_rows, a_rows), a_rows)
            scores = _dot(u_ref[ar, :], h2t_ref[...])
            for s in range(n_strips):
                a_new[s, ar, :] = scores[:, s * LANES:(s + 1) * LANES]
            dr = pl.ds(pl.multiple_of(ls * d_rows, d_rows), d_rows)
            weighted = jnp.concatenate([p_old[s] for s in range(n_strips)], axis=1)
            acc_scr[dr, :] += _dot(vt_ref[dr, :], weighted)
            cols = pl.ds(pl.multiple_of(ls * LANES, LANES), LANES)
            for il in range(ni):
                rows = slice(il * N_KEYS, (il + 1) * N_KEYS)
                w = jnp.zeros((N_KEYS, LANES), F32)
                for hh in range(PEER_HEADS):
                    s1row = s1_ref[hh, ls, il:il + 1, :]
                    e1row = e1_ref[hh, ls, il:il + 1, :]
                    sel = (s1row + s2_ref[hh, ls]) >= th_ref[hh, :, cols]
                    w = w + jnp.where(sel, e2_ref[hh, ls], 0.0) * e1row
                a = a_mid[ls, rows, :]
                act = 0.5 * a * (1.0 + lax.erf(a * (2.0 ** -0.5)))
                p_mid[ls, rows, :] = (w * act).astype(BF16)
            return carry

        lax.fori_loop(0, n_strips, per_strip, 0)

    @pl.when(e % 2 == 0)
    def _():
        step(a0_scr, a1_scr, p0_scr, p1_scr)

    @pl.when(e % 2 == 1)
    def _():
        step(a1_scr, a0_scr, p1_scr, p0_scr)

    @pl.when(e == pl.num_programs(1) - 1)
    def _():
        y_ref[...] = _rms(x1_ref[...] + acc_scr[...].T, gf_ref[...])


def _peer_dense(h2t, x1, st, e1, e2, th, u_bf, vt_bf, gf, tt, ni):
    T = x1.shape[0]
    ne = N_KEYS // ni
    et = ni * N_KEYS
    ns = tt // LANES
    tok = lambda t, e: (t, 0)
    tile = lambda e, lag: jnp.clip(e - lag, 0, ne - 1)
    return pl.pallas_call(
        functools.partial(_peer_dense_kernel, tt=tt, ni=ni),
        grid=(T // tt, ne + 2),
        in_specs=[pl.BlockSpec((D_MODEL, tt), lambda t, e: (0, t)),
                  pl.BlockSpec((tt, D_MODEL), tok),
                  pl.BlockSpec((PEER_HEADS, None, ns, ni, LANES),
                               lambda t, e: (0, 0, t, tile(e, 1), 0)),
                  pl.BlockSpec((PEER_HEADS, None, ns, N_KEYS, LANES), lambda t, e: (0, 1, t, 0, 0)),
                  pl.BlockSpec((PEER_HEADS, ns, ni, LANES), lambda t, e: (0, t, tile(e, 1), 0)),
                  pl.BlockSpec((PEER_HEADS, ns, N_KEYS, LANES), lambda t, e: (0, t, 0, 0)),
                  pl.BlockSpec((PEER_HEADS, 1, tt), lambda t, e: (0, 0, t)),
                  pl.BlockSpec((et, D_MODEL), lambda t, e: (tile(e, 0), 0)),
                  pl.BlockSpec((D_MODEL, et), lambda t, e: (0, tile(e, 2))),
                  pl.BlockSpec((1, D_MODEL), lambda t, e: (0, 0))],
        out_specs=pl.BlockSpec((tt, D_MODEL), tok),
        out_shape=jax.ShapeDtypeStruct((T, D_MODEL), F32),
        scratch_shapes=[pltpu.VMEM((ns, et, LANES), F32),
                        pltpu.VMEM((ns, et, LANES), F32),
                        pltpu.VMEM((ns, et, LANES), BF16),
                        pltpu.VMEM((ns, et, LANES), BF16),
                        pltpu.VMEM((D_MODEL, tt), F32)],
        compiler_params=_params("parallel", "arbitrary"),
        name="peer_dense",
    )(h2t, x1, st, st, e1, e2, th, u_bf, vt_bf, gf)


def _rope_tables(pos):
    half = ROT_DIM // 2
    inv = jnp.power(ROPE_THETA, -jnp.arange(half, dtype=F32) * 2.0 / ROT_DIM)
    ang = pos.astype(F32)[:, None] * inv[None, :]
    cos, sin = jnp.cos(ang), jnp.sin(ang)
    n = pos.shape[0]
    pad = jnp.zeros((n, DA_HD - ROT_DIM), F32)
    zero = jnp.zeros((n, half), F32)
    cos_t = jnp.concatenate([cos, cos, pad + 1.0], axis=1)
    sin_a = jnp.concatenate([-sin, zero, pad], axis=1)
    sin_b = jnp.concatenate([zero, sin, pad], axis=1)
    return tuple(jnp.tile(t, (1, LANES // DA_HD)) for t in (cos_t, sin_a, sin_b))


def _trunk(x, pos, past_k, past_v, s0, chunk, wts, layer, tm, tq, gla_rows, tt, ni):
    B, L, _ = x.shape
    T = B * L
    lam_init = 0.8 - 0.6 * math.exp(-0.3 * layer)
    x2d = x.reshape(T, D_MODEL)
    cos_t, sin_a, sin_b = _rope_tables(pos)
    if L < tm:
        cos_t, sin_a, sin_b = (jnp.tile(t, (tm // L, 1)) for t in (cos_t, sin_a, sin_b))

    q_bf, k32, k_bf, v32, v_bf, siga = _da_proj(
        x2d, wts["g1"], wts["w_qkv"], wts["w_ga"], cos_t, sin_a, sin_b, tm,
        transpose_v=past_k is None)
    if past_k is None:
        oda = _attn_prompt(wts["lam4"], q_bf, k_bf, v_bf, siga, wts["subg"], B, L, tq, lam_init)
    else:
        oda = _attn_sample(wts["lam4"], q_bf, past_k, past_v, k_bf, v_bf, siga, wts["subg"],
                           B, L, lam_init)
    merged, s_fin = _gla(x2d, wts["g1"], wts["w_g"], wts["w_lr"], wts["w_a"], wts["b_a"],
                         wts["gla_g"], s0, oda, B, L, gla_rows, chunk)
    x1, h2t, st = _peer_query(x2d, merged, wts["w_out"], wts["g2"], wts["wq"], wts["subkeys"], tm)
    st = st.reshape(PEER_HEADS, 2, T // LANES, N_KEYS, LANES)
    e1, e2, th = _peer_route(st, tt)
    y = _peer_dense(h2t, x1, st, e1, e2, th, wts["u"], wts["vt"], wts["gf"], tt, ni)
    return (y.reshape(B, L, D_MODEL), k32.reshape(1, B, L, DA_HEADS, 2, DA_HD),
            v32.reshape(1, B, L, DA_HEADS, DA_VD), s_fin[None])


def kernel(x_prompt, x_sample, cache_da_k, cache_da_v, state_gla, norm1_g, w_in, da_lambda_q1,
           da_lambda_k1, da_lambda_q2, da_lambda_k2, da_subln_g, gla_w_alpha, gla_b_alpha,
           gla_norm_g, w_out, norm2_g, peer_wq, peer_subkeys, peer_u, peer_v, final_norm_g):
    layer = 0
    B, L, _ = x_prompt.shape
    SB, SL, _ = x_sample.shape
    P = cache_da_k.shape[2]

    o_dq, o_gq = 0, 3 * D_MODEL
    o_gr = o_gq + 2 * GLA_K + GLA_V
    o_lr = o_gr + GLA_V
    o_ga = o_lr + GLA_GATE_RANK
    o_gb = o_ga + D_MODEL
    w = w_in[layer]
    pad_rank = LANES - GLA_GATE_RANK
    wts = {
        "g1": norm1_g[layer][None, :],
        "w_qkv": w[:, o_dq:o_gq].astype(BF16),
        "w_ga": w[:, o_ga:o_gb].astype(BF16),
        "w_g": jnp.concatenate([w[:, o_gq:o_lr], w[:, o_gb:]], axis=1).astype(BF16),
        "w_lr": jnp.pad(w[:, o_lr:o_ga], ((0, 0), (0, pad_rank))).astype(BF16),
        "w_a": jnp.pad(gla_w_alpha[layer], ((0, pad_rank), (0, 0))).astype(BF16),
        "b_a": gla_b_alpha[layer][None, :],
        "gla_g": gla_norm_g[layer][None, :],
        "lam4": jnp.stack([da_lambda_q1[layer], da_lambda_k1[layer],
                           da_lambda_q2[layer], da_lambda_k2[layer]]),
        "subg": da_subln_g[layer][None, :],
        "w_out": w_out[layer].astype(BF16),
        "g2": norm2_g[layer][None, :],
        "wq": peer_wq[layer].astype(BF16),
        "subkeys": peer_subkeys[layer].reshape(2 * PEER_HEADS, N_KEYS, PK_DIM).astype(BF16),
        "u": peer_u[layer].astype(BF16),
        "vt": peer_v[layer].astype(BF16).T,
        "gf": final_norm_g[None, :],
    }

    pos_p = jnp.arange(L, dtype=F32)
    pos_s = P + jnp.arange(SL, dtype=F32)
    zeros_state = jnp.zeros((B, GLA_HEADS, GLA_HK, GLA_HV), F32)
    tm_p = min(256, B * L)
    y_p, k_p, v_p, s_p = _trunk(x_prompt, pos_p, None, None, zeros_state, CHUNK, wts, layer,
                                tm=tm_p, tq=min(256, L), gla_rows=min(256, L),
                                tt=min(512, B * L), ni=8)
    past_k = cache_da_k[layer].reshape(SB, P, D_MODEL)
    past_v = cache_da_v[layer].reshape(SB, P, D_MODEL)
    y_s, k_s, v_s, s_s = _trunk(x_sample, pos_s, past_k, past_v, state_gla[layer], SL, wts, layer,
                                tm=SB * SL, tq=SL, gla_rows=SL, tt=SB * SL, ni=8)
    return (y_p, y_s, k_p, v_p, s_p, k_s, v_s, s_s)
```

```python
import functools
import math

import jax
import jax.numpy as jnp
from jax import lax
from jax.experimental import pallas as pl
from jax.experimental.pallas import tpu as pltpu

F32 = jnp.float32
BF16 = jnp.bfloat16

D_MODEL = 1024
CHUNK = 64
NORM_EPS = 1e-6

DA_HEADS = 8
DA_HD = 64
DA_VD = 128
ROT_DIM = DA_HD // 4
ROPE_THETA = 500000.0
Q_SCALE = DA_HD ** -0.5 * math.log2(math.e)

GLA_HEADS = 4
GLA_K = 512
GLA_V = 1024
GLA_HK = 128
GLA_HV = 256
GLA_GATE_RANK = 16
GLA_GATE_TAU = 16.0

PEER_HEADS = 8
N_KEYS = 128
PK_DIM = 128
PEER_TOPK = 16

LANES = 128
VMEM_LIMIT = 56 * 1024 * 1024

NEG_INF = float("-inf")


def _params(*sem):
    return pltpu.CompilerParams(dimension_semantics=sem, vmem_limit_bytes=VMEM_LIMIT)


def _rms(xf, g):
    ms = jnp.mean(xf * xf, axis=-1, keepdims=True)
    return xf * lax.rsqrt(ms + NORM_EPS) * g


def _dot(a, b):
    return jnp.dot(a, b, preferred_element_type=F32)


def _dot_nt(a, b):
    return lax.dot_general(a, b, (((1,), (1,)), ((), ())), preferred_element_type=F32)


def _dot_tn(a, b):
    return lax.dot_general(a, b, (((0,), (0,)), ((), ())), preferred_element_type=F32)


def _da_proj_kernel(x_ref, g_ref, w_ref, wga_ref, cos_ref, sa_ref, sb_ref,
                    q_ref, k32_ref, kbf_ref, v32_ref, vbf_ref, siga_ref, *, transpose_v):
    h = _rms(x_ref[...], g_ref[...]).astype(BF16)
    cos_t = cos_ref[...]
    sin_a = sa_ref[...]
    sin_b = sb_ref[...]

    def rope(blk):
        return (blk * cos_t + pltpu.roll(blk, LANES - ROT_DIM // 2, 1) * sin_a
                + pltpu.roll(blk, ROT_DIM // 2, 1) * sin_b)

    qf = _dot(h, w_ref[:, 0:D_MODEL])
    for c in range(D_MODEL // LANES):
        sl = slice(c * LANES, (c + 1) * LANES)
        q_ref[:, sl] = (rope(qf[:, sl]) * Q_SCALE).astype(BF16)
    kf = _dot(h, w_ref[:, D_MODEL:2 * D_MODEL])
    for c in range(D_MODEL // LANES):
        sl = slice(c * LANES, (c + 1) * LANES)
        kr = rope(kf[:, sl])
        k32_ref[:, sl] = kr
        kbf_ref[:, sl] = kr.astype(BF16)
    vf = _dot(h, w_ref[:, 2 * D_MODEL:3 * D_MODEL])
    v32_ref[...] = vf
    vbf_ref[...] = (vf.T if transpose_v else vf).astype(BF16)
    siga_ref[...] = jax.nn.sigmoid(_dot(h, wga_ref[...]))


def _da_proj(x2d, g1, w_qkv, w_ga, cos_t, sin_a, sin_b, tm, transpose_v):
    T = x2d.shape[0]
    n_pos_blocks = cos_t.shape[0] // tm
    row = lambda i: (i, 0)
    fixed = lambda i: (0, 0)
    pos = lambda i: (i % n_pos_blocks, 0)
    full = pl.BlockSpec((tm, D_MODEL), row)
    tab = pl.BlockSpec((tm, LANES), pos)
    if transpose_v:
        v_spec = pl.BlockSpec((D_MODEL, tm), lambda i: (0, i))
        v_shape = jax.ShapeDtypeStruct((D_MODEL, T), BF16)
    else:
        v_spec, v_shape = full, jax.ShapeDtypeStruct((T, D_MODEL), BF16)
    return pl.pallas_call(
        functools.partial(_da_proj_kernel, transpose_v=transpose_v),
        grid=(T // tm,),
        in_specs=[full, pl.BlockSpec((1, D_MODEL), fixed),
                  pl.BlockSpec((D_MODEL, 3 * D_MODEL), fixed),
                  pl.BlockSpec((D_MODEL, D_MODEL), fixed), tab, tab, tab],
        out_specs=[full, full, full, full, v_spec, full],
        out_shape=[jax.ShapeDtypeStruct((T, D_MODEL), BF16),
                   jax.ShapeDtypeStruct((T, D_MODEL), F32),
                   jax.ShapeDtypeStruct((T, D_MODEL), BF16),
                   jax.ShapeDtypeStruct((T, D_MODEL), F32),
                   v_shape,
                   jax.ShapeDtypeStruct((T, D_MODEL), F32)],
        compiler_params=_params("parallel"),
        name="da_proj",
    )(x2d, g1, w_qkv, w_ga, cos_t, sin_a, sin_b)


def _lambda_of(lam_ref, lam_init):
    a = lam_ref[...]
    d1 = jnp.sum(a[0:1, :] * a[1:2, :], axis=-1, keepdims=True)
    d2 = jnp.sum(a[2:3, :] * a[3:4, :], axis=-1, keepdims=True)
    return jnp.exp(d1) - jnp.exp(d2) + lam_init


def _stack_maps(q):
    lane = lax.broadcasted_iota(jnp.int32, q.shape, 1)
    zero = jnp.zeros_like(q)
    return jnp.concatenate([jnp.where(lane < DA_HD, q, zero),
                            jnp.where(lane >= DA_HD, q, zero)], axis=0)


def _softmax_step(qq, kb, vb, carry, mask):
    m, l, acc = carry
    s = _dot_nt(qq, kb)
    if mask is not None:
        s = jnp.where(mask, s, NEG_INF)
    m_new = jnp.maximum(m, jnp.max(s, axis=1, keepdims=True))
    alpha = jnp.exp2(m - m_new)
    p = jnp.exp2(s - m_new)
    l = alpha * l + jnp.sum(p, axis=1, keepdims=True)
    acc = alpha * acc + _dot(p.astype(BF16), vb)
    return m_new, l, acc


def _da_finish(carry, tq, lam, lam_init, subg, siga):
    _, l, acc = carry
    o = acc[:tq] / l[:tq] - lam * (acc[tq:] / l[tq:])
    return _rms(o, subg) * (1.0 - lam_init) * siga


def _softmax_update(s, m, l):
    m_new = jnp.maximum(m, jnp.max(s, axis=0, keepdims=True))
    alpha = jnp.exp2(m - m_new)
    p = jnp.exp2(s - m_new)
    return m_new, alpha * l + jnp.sum(p, axis=0, keepdims=True), alpha, p.astype(BF16)


def _attn_prompt_kernel(lam_ref, q_ref, k_ref, vt_ref, siga_ref, subg_ref, o_ref,
                        s_scr, p_scr, acc_scr, *, tq, nh, lam_init):
    i = pl.program_id(2)
    hs = [slice(h * LANES, (h + 1) * LANES) for h in range(nh)]
    qq = [_stack_maps(q_ref[:, s]) for s in hs]

    def scores(h, blk):
        return _dot_nt(k_ref[pl.ds(pl.multiple_of(blk * tq, tq), tq), hs[h]], qq[h])

    def values(h, blk):
        return _dot(vt_ref[hs[h], pl.ds(pl.multiple_of(blk * tq, tq), tq)], p_scr[h])

    def advance(h, j, stats, mask):
        m, l, a_prev = stats
        acc_scr[h] = a_prev * acc_scr[h] + values(h, jnp.maximum(j - 1, 0))
        s = s_scr[h]
        if mask is not None:
            s = jnp.where(mask, s, NEG_INF)
        m, l, alpha, p = _softmax_update(s, m, l)
        p_scr[h] = p
        return m, l, alpha

    def trip(j, stats):
        out = []
        for h in range(nh):
            out.append(advance(h, j, stats[h], None))
            s_scr[h] = scores(h, j + 1)
        return tuple(out)

    for h in range(nh):
        s_scr[h] = scores(h, 0)
        p_scr[h] = jnp.zeros((tq, 2 * tq), BF16)
        acc_scr[h] = jnp.zeros((DA_VD, 2 * tq), F32)
    init = (jnp.full((1, 2 * tq), NEG_INF, F32), jnp.zeros((1, 2 * tq), F32),
            jnp.ones((1, 2 * tq), F32))
    stats = lax.fori_loop(0, i, trip, (init,) * nh)

    key = lax.broadcasted_iota(jnp.int32, (tq, 2 * tq), 0)
    qry = lax.broadcasted_iota(jnp.int32, (tq, 2 * tq), 1)
    qry = jnp.where(qry >= tq, qry - tq, qry)
    mask = (key // CHUNK) <= (qry // CHUNK)
    lam = _lambda_of(lam_ref, lam_init)
    for h in range(nh):
        _, l, alpha = advance(h, i, stats[h], mask)
        acc = alpha * acc_scr[h] + values(h, i)
        o_t = acc[:, :tq] / l[:, :tq] - lam * (acc[:, tq:] / l[:, tq:])
        o_ref[:, hs[h]] = _rms(o_t.T, subg_ref[...]) * (1.0 - lam_init) * siga_ref[:, hs[h]]


def _attn_prompt(lam4, q_bf, k_bf, vt_bf, siga, subg, B, L, tq, lam_init, nh=2):
    nq = L // tq
    qmap = lambda b, h, i: (b * nq + i, h)
    fixed = lambda b, h, i: (0, 0)
    return pl.pallas_call(
        functools.partial(_attn_prompt_kernel, tq=tq, nh=nh, lam_init=lam_init),
        grid=(B, DA_HEADS // nh, nq),
        in_specs=[pl.BlockSpec((4, DA_HD), fixed),
                  pl.BlockSpec((tq, nh * LANES), qmap),
                  pl.BlockSpec((L, nh * LANES), lambda b, h, i: (b, h)),
                  pl.BlockSpec((nh * DA_VD, L), lambda b, h, i: (h, b)),
                  pl.BlockSpec((tq, nh * LANES), qmap),
                  pl.BlockSpec((1, DA_VD), fixed)],
        out_specs=pl.BlockSpec((tq, nh * LANES), qmap),
        out_shape=jax.ShapeDtypeStruct((B * L, D_MODEL), F32),
        scratch_shapes=[pltpu.VMEM((nh, tq, 2 * tq), F32),
                        pltpu.VMEM((nh, tq, 2 * tq), BF16),
                        pltpu.VMEM((nh, DA_VD, 2 * tq), F32)],
        compiler_params=_params("parallel", "parallel", "arbitrary"),
        name="attn_prompt",
    )(lam4, q_bf, k_bf, vt_bf, siga, subg)


def _attn_sample_kernel(lam_ref, q_ref, pk_ref, pv_ref, k_ref, v_ref, siga_ref, subg_ref, o_ref,
                        *, tq, lam_init):
    qq = _stack_maps(q_ref[...])
    init = (jnp.full((2 * tq, 1), NEG_INF, F32), jnp.zeros((2 * tq, 1), F32),
            jnp.zeros((2 * tq, DA_VD), F32))
    carry = _softmax_step(qq, pk_ref[...].astype(BF16), pv_ref[...].astype(BF16), init, None)
    carry = _softmax_step(qq, k_ref[...], v_ref[...], carry, None)
    o_ref[...] = _da_finish(carry, tq, _lambda_of(lam_ref, lam_init), lam_init,
                            subg_ref[...], siga_ref[...])


def _attn_sample(lam4, q_bf, past_k, past_v, k_bf, v_bf, siga, subg, B, L, lam_init):
    P = past_k.shape[1]
    qmap = lambda b, h: (b, h)
    pmap = lambda b, h: (b, 0, h)
    fixed = lambda b, h: (0, 0)
    return pl.pallas_call(
        functools.partial(_attn_sample_kernel, tq=L, lam_init=lam_init),
        grid=(B, DA_HEADS),
        in_specs=[pl.BlockSpec((4, DA_HD), fixed),
                  pl.BlockSpec((L, LANES), qmap),
                  pl.BlockSpec((None, P, LANES), pmap),
                  pl.BlockSpec((None, P, LANES), pmap),
                  pl.BlockSpec((L, LANES), qmap),
                  pl.BlockSpec((L, LANES), qmap),
                  pl.BlockSpec((L, LANES), qmap),
                  pl.BlockSpec((1, DA_VD), fixed)],
        out_specs=pl.BlockSpec((L, LANES), qmap),
        out_shape=jax.ShapeDtypeStruct((B * L, D_MODEL), F32),
        compiler_params=_params("parallel", "parallel"),
        name="attn_sample",
    )(lam4, q_bf, past_k, past_v, k_bf, v_bf, siga, subg)


def _gla_kernel(x_ref, g1_ref, wg_ref, wlr_ref, wa_ref, ba_ref, glag_ref, s0_ref, oda_ref,
                merged_ref, sfin_ref, st_scr, *, rows, chunk):
    t = pl.program_id(1)

    @pl.when(t == 0)
    def _():
        for hh in range(GLA_HEADS):
            st_scr[hh] = s0_ref[hh].T

    h = _rms(x_ref[...], g1_ref[...]).astype(BF16)
    proj = _dot(h, wg_ref[...])
    gq = proj[:, 0:GLA_K]
    gk = proj[:, GLA_K:2 * GLA_K]
    gv = proj[:, 2 * GLA_K:2 * GLA_K + GLA_V]
    gr = proj[:, 2 * GLA_K + GLA_V:2 * GLA_K + 2 * GLA_V]
    gb = proj[:, 2 * GLA_K + 2 * GLA_V:]
    g_lr = _dot(h, wlr_ref[...])
    z = _dot(g_lr.astype(BF16), wa_ref[...]) + ba_ref[...]
    log_a = (jnp.minimum(z, 0.0) - jnp.log1p(jnp.exp(-jnp.abs(z)))) / GLA_GATE_TAU

    ri = lax.broadcasted_iota(jnp.int32, (chunk, chunk), 0)
    ci = lax.broadcasted_iota(jnp.int32, (chunk, chunk), 1)
    causal = ci <= ri
    tri = causal.astype(F32)

    outs = []
    for c in range(rows // chunk):
        rs = slice(c * chunk, (c + 1) * chunk)
        b = jnp.dot(tri, log_a[rs], precision=lax.Precision.HIGHEST, preferred_element_type=F32)
        b_last = b[chunk - 1:chunk, :]
        e_b = jnp.exp(b)
        e_nb = jnp.exp(-b)
        e_kl = jnp.exp(b_last - b)
        e_last = jnp.exp(b_last)
        heads = []
        for hh in range(GLA_HEADS):
            ks = slice(hh * GLA_HK, (hh + 1) * GLA_HK)
            vs = slice(hh * GLA_HV, (hh + 1) * GLA_HV)
            qd = (gq[rs, ks] * (GLA_HK ** -0.5) * e_b[:, ks]).astype(BF16)
            kd = (gk[rs, ks] * e_nb[:, ks]).astype(BF16)
            kl = (gk[rs, ks] * e_kl[:, ks]).astype(BF16)
            vv = gv[rs, vs].astype(BF16)
            att = jnp.where(causal, _dot_nt(qd, kd), 0.0)
            st = st_scr[hh]
            heads.append(_dot(att.astype(BF16), vv) + _dot_nt(qd, st.astype(BF16)))
            st_scr[hh] = st * e_last[:, ks] + _dot_tn(vv, kl)
        outs.append(jnp.concatenate(heads, axis=1))
    o = jnp.concatenate(outs, axis=0) if len(outs) > 1 else outs[0]

    normed = []
    for hh in range(GLA_HEADS):
        vs = slice(hh * GLA_HV, (hh + 1) * GLA_HV)
        normed.append(_rms(o[:, vs], glag_ref[...]))
    o = jnp.concatenate(normed, axis=1) * (gr * jax.nn.sigmoid(gr))
    merged_ref[...] = (oda_ref[...] + jax.nn.sigmoid(gb) * o).astype(BF16)

    @pl.when(t == pl.num_programs(1) - 1)
    def _():
        for hh in range(GLA_HEADS):
            sfin_ref[hh] = st_scr[hh].T


def _gla(x2d, g1, w_g, w_lr, w_a, b_a, gla_g, s0, oda, B, L, rows, chunk):
    nt = L // rows
    rmap = lambda b, t: (b * nt + t, 0)
    fixed = lambda b, t: (0, 0)
    smap = lambda b, t: (b, 0, 0, 0)
    return pl.pallas_call(
        functools.partial(_gla_kernel, rows=rows, chunk=chunk),
        grid=(B, nt),
        in_specs=[pl.BlockSpec((rows, D_MODEL), rmap),
                  pl.BlockSpec((1, D_MODEL), fixed),
                  pl.BlockSpec(w_g.shape, fixed),
                  pl.BlockSpec(w_lr.shape, fixed),
                  pl.BlockSpec(w_a.shape, fixed),
                  pl.BlockSpec((1, GLA_K), fixed),
                  pl.BlockSpec((1, GLA_HV), fixed),
                  pl.BlockSpec((None, GLA_HEADS, GLA_HK, GLA_HV), smap),
                  pl.BlockSpec((rows, D_MODEL), rmap)],
        out_specs=[pl.BlockSpec((rows, D_MODEL), rmap),
                   pl.BlockSpec((None, GLA_HEADS, GLA_HK, GLA_HV), smap)],
        out_shape=[jax.ShapeDtypeStruct((B * L, D_MODEL), BF16),
                   jax.ShapeDtypeStruct((B, GLA_HEADS, GLA_HK, GLA_HV), F32)],
        scratch_shapes=[pltpu.VMEM((GLA_HEADS, GLA_HV, GLA_HK), F32)],
        compiler_params=_params("parallel", "arbitrary"),
        name="gla",
    )(x2d, g1, w_g, w_lr, w_a, b_a, gla_g, s0, oda)


def _peer_query_kernel(x_ref, m_ref, wo_ref, g2_ref, wq_ref, sk_ref, x1_ref, h2t_ref, st_ref):
    x1 = x_ref[...] + _dot(m_ref[...], wo_ref[...])
    x1_ref[...] = x1
    h2f = _rms(x1, g2_ref[...])
    h2t_ref[...] = h2f.T.astype(BF16)
    q = _dot(h2f.astype(BF16), wq_ref[...]).astype(BF16)
    for j in range(2 * PEER_HEADS):
        s = _dot_nt(sk_ref[j], q[:, j * PK_DIM:(j + 1) * PK_DIM])
        for strip in range(s.shape[1] // LANES):
            st_ref[j, strip] = s[:, strip * LANES:(strip + 1) * LANES]


def _peer_query(x2d, merged, w_out, g2, wq, subkeys, tm):
    T = x2d.shape[0]
    row = lambda i: (i, 0)
    fixed = lambda i: (0, 0)
    nj = 2 * PEER_HEADS
    return pl.pallas_call(
        _peer_query_kernel,
        grid=(T // tm,),
        in_specs=[pl.BlockSpec((tm, D_MODEL), row),
                  pl.BlockSpec((tm, D_MODEL), row),
                  pl.BlockSpec((D_MODEL, D_MODEL), fixed),
                  pl.BlockSpec((1, D_MODEL), fixed),
                  pl.BlockSpec((D_MODEL, nj * PK_DIM), fixed),
                  pl.BlockSpec((nj, N_KEYS, PK_DIM), lambda i: (0, 0, 0))],
        out_specs=[pl.BlockSpec((tm, D_MODEL), row),
                   pl.BlockSpec((D_MODEL, tm), lambda i: (0, i)),
                   pl.BlockSpec((nj, tm // LANES, N_KEYS, LANES), lambda i: (0, i, 0, 0))],
        out_shape=[jax.ShapeDtypeStruct((T, D_MODEL), F32),
                   jax.ShapeDtypeStruct((D_MODEL, T), BF16),
                   jax.ShapeDtypeStruct((nj, T // LANES, N_KEYS, LANES), F32)],
        compiler_params=_params("parallel"),
        name="peer_query",
    )(x2d, merged, w_out, g2, wq, subkeys)


def _top_sorted(s, k):
    work = s
    rows = []
    for _ in range(k):
        m = jnp.max(work, axis=0, keepdims=True)
        rows.append(m)
        work = jnp.where(work == m, NEG_INF, work)
    return jnp.concatenate(rows, axis=0)


def _peer_route_kernel(st_ref, e1_ref, e2_ref, th_ref):
    for strip in range(st_ref.shape[1]):
        e1, e2, theta = _route_strip(st_ref[0, strip], st_ref[1, strip])
        e1_ref[strip] = e1
        e2_ref[strip] = e2
        th_ref[:, strip * LANES:(strip + 1) * LANES] = theta


def _route_strip(s1, s2):
    a = _top_sorted(s1, PEER_TOPK)
    b = _top_sorted(s2, PEER_TOPK)
    half = PEER_TOPK // 2
    cands = [a[0:half] + b[0:1], a[half:] + b[0:1], a[0:1] + b[half:]]
    for c in range(1, half):
        cands.append(a[0:half] + b[c:c + 1])
    cand = jnp.concatenate(cands, axis=0)
    work = cand
    for _ in range(PEER_TOPK):
        theta = jnp.max(work, axis=0, keepdims=True)
        work = jnp.where(work == theta, NEG_INF, work)
    top = a[0:1] + b[0:1]
    zsum = jnp.sum(jnp.where(cand >= theta, jnp.exp(cand - top), 0.0), axis=0, keepdims=True)
    return jnp.exp(s1 - a[0:1]), jnp.exp(s2 - b[0:1]) / zsum, theta


def _peer_route(st, tt):
    ns_all = st.shape[2]
    ns = tt // LANES
    tile = pl.BlockSpec((None, ns, N_KEYS, LANES), lambda h, i: (h, i, 0, 0))
    return pl.pallas_call(
        _peer_route_kernel,
        grid=(PEER_HEADS, ns_all // ns),
        in_specs=[pl.BlockSpec((None, 2, ns, N_KEYS, LANES), lambda h, i: (h, 0, i, 0, 0))],
        out_specs=[tile, tile, pl.BlockSpec((None, 1, tt), lambda h, i: (h, 0, i))],
        out_shape=[jax.ShapeDtypeStruct((PEER_HEADS, ns_all, N_KEYS, LANES), F32),
                   jax.ShapeDtypeStruct((PEER_HEADS, ns_all, N_KEYS, LANES), F32),
                   jax.ShapeDtypeStruct((PEER_HEADS, 1, ns_all * LANES), F32)],
        compiler_params=_params("parallel", "parallel"),
        name="peer_route",
    )(st)


def _pack_tables_kernel(u_ref, v_ref, upk_ref, vtpk_ref):
    upk_ref[...] = pltpu.bitcast(u_ref[...].astype(BF16), jnp.uint32)
    vtpk_ref[...] = pltpu.bitcast(v_ref[...].T.astype(BF16), jnp.uint32)


def _pack_tables(u, v, rows):
    n = u.shape[0]
    return pl.pallas_call(
        _pack_tables_kernel,
        grid=(n // rows,),
        in_specs=[pl.BlockSpec((rows, D_MODEL), lambda i: (i, 0)),
                  pl.BlockSpec((rows, D_MODEL), lambda i: (i, 0))],
        out_specs=[pl.BlockSpec((rows // 2, D_MODEL), lambda i: (i, 0)),
                   pl.BlockSpec((D_MODEL // 2, rows), lambda i: (0, i))],
        out_shape=[jax.ShapeDtypeStruct((n // 2, D_MODEL), jnp.uint32),
                   jax.ShapeDtypeStruct((D_MODEL // 2, n), jnp.uint32)],
        compiler_params=_params("parallel"),
        name="pack_tables",
    )(u, v)


def _peer_dense_kernel(h2t_ref, x1_ref, s1_ref, s2_ref, e1_ref, e2_ref, th_ref, u_ref, vt_ref, gf_ref,
                       y_ref, a0_scr, a1_scr, p0_scr, p1_scr, acc_scr, *, tt, ni):
    e = pl.program_id(1)
    n_strips = tt // LANES
    a_rows = ni * N_KEYS // n_strips
    d_rows = D_MODEL // n_strips

    @pl.when(e == 0)
    def _():
        for ref in (a0_scr, a1_scr, p0_scr, p1_scr, acc_scr):
            ref[...] = jnp.zeros_like(ref)

    def step(a_new, a_mid, p_old, p_mid):

        def per_strip(ls, carry):
            ar = pl.ds(pl.multiple_of(ls * a_rows, a_rows), a_rows)
            u_rows = pl.ds(pl.multiple_of(ls * (a_rows // 2), a_rows // 2), a_rows // 2)
            scores = _dot(pltpu.bitcast(u_ref[u_rows, :], BF16), h2t_ref[...])
            for s in range(n_strips):
                a_new[s, ar, :] = scores[:, s * LANES:(s + 1) * LANES]
            dr = pl.ds(pl.multiple_of(ls * d_rows, d_rows), d_rows)
            v_rows = pl.ds(pl.multiple_of(ls * (d_rows // 2), d_rows // 2), d_rows // 2)
            weighted = jnp.concatenate([p_old[s] for s in range(n_strips)], axis=1)
            acc_scr[dr, :] += _dot(pltpu.bitcast(vt_ref[v_rows, :], BF16), weighted)
            cols = pl.ds(pl.multiple_of(ls * LANES, LANES), LANES)
            for il in range(ni):
                rows = slice(il * N_KEYS, (il + 1) * N_KEYS)
                w = jnp.zeros((N_KEYS, LANES), F32)
                for hh in range(PEER_HEADS):
                    s1row = s1_ref[hh, ls, il:il + 1, :]
                    e1row = e1_ref[hh, ls, il:il + 1, :]
                    sel = (s1row + s2_ref[hh, ls]) >= th_ref[hh, :, cols]
                    w = w + jnp.where(sel, e2_ref[hh, ls], 0.0) * e1row
                a = a_mid[ls, rows, :]
                act = 0.5 * a * (1.0 + lax.erf(a * (2.0 ** -0.5)))
                p_mid[ls, rows, :] = (w * act).astype(BF16)
            return carry

        lax.fori_loop(0, n_strips, per_strip, 0)

    @pl.when(e % 2 == 0)
    def _():
        step(a0_scr, a1_scr, p0_scr, p1_scr)

    @pl.when(e % 2 == 1)
    def _():
        step(a1_scr, a0_scr, p1_scr, p0_scr)

    @pl.when(e == pl.num_programs(1) - 1)
    def _():
        y_ref[...] = _rms(x1_ref[...] + acc_scr[...].T, gf_ref[...])


def _peer_dense(h2t, x1, st, e1, e2, th, u_bf, vt_bf, gf, tt, ni):
    T = x1.shape[0]
    ne = N_KEYS // ni
    et = ni * N_KEYS
    ns = tt // LANES
    tok = lambda t, e: (t, 0)
    tile = lambda e, lag: jnp.clip(e - lag, 0, ne - 1)
    return pl.pallas_call(
        functools.partial(_peer_dense_kernel, tt=tt, ni=ni),
        grid=(T // tt, ne + 2),
        in_specs=[pl.BlockSpec((D_MODEL, tt), lambda t, e: (0, t)),
                  pl.BlockSpec((tt, D_MODEL), tok),
                  pl.BlockSpec((PEER_HEADS, None, ns, ni, LANES),
                               lambda t, e: (0, 0, t, tile(e, 1), 0)),
                  pl.BlockSpec((PEER_HEADS, None, ns, N_KEYS, LANES), lambda t, e: (0, 1, t, 0, 0)),
                  pl.BlockSpec((PEER_HEADS, ns, ni, LANES), lambda t, e: (0, t, tile(e, 1), 0)),
                  pl.BlockSpec((PEER_HEADS, ns, N_KEYS, LANES), lambda t, e: (0, t, 0, 0)),
                  pl.BlockSpec((PEER_HEADS, 1, tt), lambda t, e: (0, 0, t)),
                  pl.BlockSpec((et // 2, D_MODEL), lambda t, e: (tile(e, 0), 0)),
                  pl.BlockSpec((D_MODEL // 2, et), lambda t, e: (0, tile(e, 2))),
                  pl.BlockSpec((1, D_MODEL), lambda t, e: (0, 0))],
        out_specs=pl.BlockSpec((tt, D_MODEL), tok),
        out_shape=jax.ShapeDtypeStruct((T, D_MODEL), F32),
        scratch_shapes=[pltpu.VMEM((ns, et, LANES), F32),
                        pltpu.VMEM((ns, et, LANES), F32),
                        pltpu.VMEM((ns, et, LANES), BF16),
                        pltpu.VMEM((ns, et, LANES), BF16),
                        pltpu.VMEM((D_MODEL, tt), F32)],
        compiler_params=_params("parallel", "arbitrary"),
        name="peer_dense",
    )(h2t, x1, st, st, e1, e2, th, u_bf, vt_bf, gf)


def _rope_tables(pos):
    half = ROT_DIM // 2
    inv = jnp.power(ROPE_THETA, -jnp.arange(half, dtype=F32) * 2.0 / ROT_DIM)
    ang = pos.astype(F32)[:, None] * inv[None, :]
    cos, sin = jnp.cos(ang), jnp.sin(ang)
    n = pos.shape[0]
    pad = jnp.zeros((n, DA_HD - ROT_DIM), F32)
    zero = jnp.zeros((n, half), F32)
    cos_t = jnp.concatenate([cos, cos, pad + 1.0], axis=1)
    sin_a = jnp.concatenate([-sin, zero, pad], axis=1)
    sin_b = jnp.concatenate([zero, sin, pad], axis=1)
    return tuple(jnp.tile(t, (1, LANES // DA_HD)) for t in (cos_t, sin_a, sin_b))


def _trunk(x, pos, past_k, past_v, s0, chunk, wts, layer, tm, tq, gla_rows, tt, ni):
    B, L, _ = x.shape
    T = B * L
    lam_init = 0.8 - 0.6 * math.exp(-0.3 * layer)
    x2d = x.reshape(T, D_MODEL)
    cos_t, sin_a, sin_b = _rope_tables(pos)
    if L < tm:
        cos_t, sin_a, sin_b = (jnp.tile(t, (tm // L, 1)) for t in (cos_t, sin_a, sin_b))

    q_bf, k32, k_bf, v32, v_bf, siga = _da_proj(
        x2d, wts["g1"], wts["w_qkv"], wts["w_ga"], cos_t, sin_a, sin_b, tm,
        transpose_v=past_k is None)
    if past_k is None:
        oda = _attn_prompt(wts["lam4"], q_bf, k_bf, v_bf, siga, wts["subg"], B, L, tq, lam_init)
    else:
        oda = _attn_sample(wts["lam4"], q_bf, past_k, past_v, k_bf, v_bf, siga, wts["subg"],
                           B, L, lam_init)
    merged, s_fin = _gla(x2d, wts["g1"], wts["w_g"], wts["w_lr"], wts["w_a"], wts["b_a"],
                         wts["gla_g"], s0, oda, B, L, gla_rows, chunk)
    x1, h2t, st = _peer_query(x2d, merged, wts["w_out"], wts["g2"], wts["wq"], wts["subkeys"], tm)
    st = st.reshape(PEER_HEADS, 2, T // LANES, N_KEYS, LANES)
    e1, e2, th = _peer_route(st, tt)
    y = _peer_dense(h2t, x1, st, e1, e2, th, wts["u"], wts["vt"], wts["gf"], tt, ni)
    return (y.reshape(B, L, D_MODEL), k32.reshape(1, B, L, DA_HEADS, 2, DA_HD),
            v32.reshape(1, B, L, DA_HEADS, DA_VD), s_fin[None])


def kernel(x_prompt, x_sample, cache_da_k, cache_da_v, state_gla, norm1_g, w_in, da_lambda_q1,
           da_lambda_k1, da_lambda_q2, da_lambda_k2, da_subln_g, gla_w_alpha, gla_b_alpha,
           gla_norm_g, w_out, norm2_g, peer_wq, peer_subkeys, peer_u, peer_v, final_norm_g):
    layer = 0
    B, L, _ = x_prompt.shape
    SB, SL, _ = x_sample.shape
    P = cache_da_k.shape[2]

    o_dq, o_gq = 0, 3 * D_MODEL
    o_gr = o_gq + 2 * GLA_K + GLA_V
    o_lr = o_gr + GLA_V
    o_ga = o_lr + GLA_GATE_RANK
    o_gb = o_ga + D_MODEL
    w = w_in[layer]
    pad_rank = LANES - GLA_GATE_RANK
    wts = {
        "g1": norm1_g[layer][None, :],
        "w_qkv": w[:, o_dq:o_gq].astype(BF16),
        "w_ga": w[:, o_ga:o_gb].astype(BF16),
        "w_g": jnp.concatenate([w[:, o_gq:o_lr], w[:, o_gb:]], axis=1).astype(BF16),
        "w_lr": jnp.pad(w[:, o_lr:o_ga], ((0, 0), (0, pad_rank))).astype(BF16),
        "w_a": jnp.pad(gla_w_alpha[layer], ((0, pad_rank), (0, 0))).astype(BF16),
        "b_a": gla_b_alpha[layer][None, :],
        "gla_g": gla_norm_g[layer][None, :],
        "lam4": jnp.stack([da_lambda_q1[layer], da_lambda_k1[layer],
                           da_lambda_q2[layer], da_lambda_k2[layer]]),
        "subg": da_subln_g[layer][None, :],
        "w_out": w_out[layer].astype(BF16),
        "g2": norm2_g[layer][None, :],
        "wq": peer_wq[layer].astype(BF16),
        "subkeys": peer_subkeys[layer].reshape(2 * PEER_HEADS, N_KEYS, PK_DIM).astype(BF16),
        "gf": final_norm_g[None, :],
    }
    wts["u"], wts["vt"] = _pack_tables(peer_u[layer], peer_v[layer], rows=512)

    pos_p = jnp.arange(L, dtype=F32)
    pos_s = P + jnp.arange(SL, dtype=F32)
    zeros_state = jnp.zeros((B, GLA_HEADS, GLA_HK, GLA_HV), F32)
    tm_p = min(256, B * L)
    y_p, k_p, v_p, s_p = _trunk(x_prompt, pos_p, None, None, zeros_state, CHUNK, wts, layer,
                                tm=tm_p, tq=min(256, L), gla_rows=min(256, L),
                                tt=min(512, B * L), ni=8)
    past_k = cache_da_k[layer].reshape(SB, P, D_MODEL)
    past_v = cache_da_v[layer].reshape(SB, P, D_MODEL)
    y_s, k_s, v_s, s_s = _trunk(x_sample, pos_s, past_k, past_v, state_gla[layer], SL, wts, layer,
                                tm=SB * SL, tq=SL, gla_rows=SL, tt=SB * SL, ni=8)
    return (y_p, y_s, k_p, v_p, s_p, k_s, v_s, s_s)
```

```python
import functools
import math

import jax
import jax.numpy as jnp
from jax import lax
from jax.experimental import pallas as pl
from jax.experimental.pallas import tpu as pltpu

F32 = jnp.float32
BF16 = jnp.bfloat16

D_MODEL = 1024
CHUNK = 64
NORM_EPS = 1e-6

DA_HEADS = 8
DA_HD = 64
DA_VD = 128
ROT_DIM = DA_HD // 4
ROPE_THETA = 500000.0
Q_SCALE = DA_HD ** -0.5 * math.log2(math.e)

GLA_HEADS = 4
GLA_K = 512
GLA_V = 1024
GLA_HK = 128
GLA_HV = 256
GLA_GATE_RANK = 16
GLA_GATE_TAU = 16.0

PEER_HEADS = 8
N_KEYS = 128
PK_DIM = 128
PEER_TOPK = 16

LANES = 128
VMEM_LIMIT = 56 * 1024 * 1024

NEG_INF = float("-inf")


def _params(*sem):
    return pltpu.CompilerParams(dimension_semantics=sem, vmem_limit_bytes=VMEM_LIMIT)


def _rms(xf, g):
    ms = jnp.mean(xf * xf, axis=-1, keepdims=True)
    return xf * lax.rsqrt(ms + NORM_EPS) * g


def _dot(a, b):
    return jnp.dot(a, b, preferred_element_type=F32)


def _dot_nt(a, b):
    return lax.dot_general(a, b, (((1,), (1,)), ((), ())), preferred_element_type=F32)


def _dot_tn(a, b):
    return lax.dot_general(a, b, (((0,), (0,)), ((), ())), preferred_element_type=F32)


def _da_proj_kernel(x_ref, g_ref, w_ref, wga_ref, cos_ref, sa_ref, sb_ref,
                    q_ref, k32_ref, kbf_ref, v32_ref, vbf_ref, siga_ref, *, transpose_v):
    h = _rms(x_ref[...], g_ref[...]).astype(BF16)
    cos_t = cos_ref[...]
    sin_a = sa_ref[...]
    sin_b = sb_ref[...]

    def rope(blk):
        return (blk * cos_t + pltpu.roll(blk, LANES - ROT_DIM // 2, 1) * sin_a
                + pltpu.roll(blk, ROT_DIM // 2, 1) * sin_b)

    qf = _dot(h, w_ref[:, 0:D_MODEL])
    for c in range(D_MODEL // LANES):
        sl = slice(c * LANES, (c + 1) * LANES)
        q_ref[:, sl] = (rope(qf[:, sl]) * Q_SCALE).astype(BF16)
    kf = _dot(h, w_ref[:, D_MODEL:2 * D_MODEL])
    for c in range(D_MODEL // LANES):
        sl = slice(c * LANES, (c + 1) * LANES)
        kr = rope(kf[:, sl])
        k32_ref[:, sl] = kr
        kbf_ref[:, sl] = kr.astype(BF16)
    vf = _dot(h, w_ref[:, 2 * D_MODEL:3 * D_MODEL])
    v32_ref[...] = vf
    vbf_ref[...] = (vf.T if transpose_v else vf).astype(BF16)
    siga_ref[...] = jax.nn.sigmoid(_dot(h, wga_ref[...]))


def _da_proj(x2d, g1, w_qkv, w_ga, cos_t, sin_a, sin_b, tm, transpose_v):
    T = x2d.shape[0]
    n_pos_blocks = cos_t.shape[0] // tm
    row = lambda i: (i, 0)
    fixed = lambda i: (0, 0)
    pos = lambda i: (i % n_pos_blocks, 0)
    full = pl.BlockSpec((tm, D_MODEL), row)
    tab = pl.BlockSpec((tm, LANES), pos)
    if transpose_v:
        v_spec = pl.BlockSpec((D_MODEL, tm), lambda i: (0, i))
        v_shape = jax.ShapeDtypeStruct((D_MODEL, T), BF16)
    else:
        v_spec, v_shape = full, jax.ShapeDtypeStruct((T, D_MODEL), BF16)
    return pl.pallas_call(
        functools.partial(_da_proj_kernel, transpose_v=transpose_v),
        grid=(T // tm,),
        in_specs=[full, pl.BlockSpec((1, D_MODEL), fixed),
                  pl.BlockSpec((D_MODEL, 3 * D_MODEL), fixed),
                  pl.BlockSpec((D_MODEL, D_MODEL), fixed), tab, tab, tab],
        out_specs=[full, full, full, full, v_spec, full],
        out_shape=[jax.ShapeDtypeStruct((T, D_MODEL), BF16),
                   jax.ShapeDtypeStruct((T, D_MODEL), F32),
                   jax.ShapeDtypeStruct((T, D_MODEL), BF16),
                   jax.ShapeDtypeStruct((T, D_MODEL), F32),
                   v_shape,
                   jax.ShapeDtypeStruct((T, D_MODEL), F32)],
        compiler_params=_params("parallel"),
        name="da_proj",
    )(x2d, g1, w_qkv, w_ga, cos_t, sin_a, sin_b)


def _lambda_of(lam_ref, lam_init):
    a = lam_ref[...]
    d1 = jnp.sum(a[0:1, :] * a[1:2, :], axis=-1, keepdims=True)
    d2 = jnp.sum(a[2:3, :] * a[3:4, :], axis=-1, keepdims=True)
    return jnp.exp(d1) - jnp.exp(d2) + lam_init


def _stack_maps(q):
    lane = lax.broadcasted_iota(jnp.int32, q.shape, 1)
    zero = jnp.zeros_like(q)
    return jnp.concatenate([jnp.where(lane < DA_HD, q, zero),
                            jnp.where(lane >= DA_HD, q, zero)], axis=0)


def _softmax_step(qq, kb, vb, carry, mask):
    m, l, acc = carry
    s = _dot_nt(qq, kb)
    if mask is not None:
        s = jnp.where(mask, s, NEG_INF)
    m_new = jnp.maximum(m, jnp.max(s, axis=1, keepdims=True))
    alpha = jnp.exp2(m - m_new)
    p = jnp.exp2(s - m_new)
    l = alpha * l + jnp.sum(p, axis=1, keepdims=True)
    acc = alpha * acc + _dot(p.astype(BF16), vb)
    return m_new, l, acc


def _da_finish(carry, tq, lam, lam_init, subg, siga):
    _, l, acc = carry
    o = acc[:tq] / l[:tq] - lam * (acc[tq:] / l[tq:])
    return _rms(o, subg) * (1.0 - lam_init) * siga


def _softmax_update(s, m, l):
    m_new = jnp.maximum(m, jnp.max(s, axis=0, keepdims=True))
    alpha = jnp.exp2(m - m_new)
    p = jnp.exp2(s - m_new)
    return m_new, alpha * l + jnp.sum(p, axis=0, keepdims=True), alpha, p.astype(BF16)


def _attn_prompt_kernel(lam_ref, q_ref, k_ref, vt_ref, siga_ref, subg_ref, o_ref,
                        s_scr, p_scr, acc_scr, *, tq, nh, lam_init):
    i = pl.program_id(2)
    hs = [slice(h * LANES, (h + 1) * LANES) for h in range(nh)]
    qq = [_stack_maps(q_ref[:, s]) for s in hs]

    def scores(h, blk):
        return _dot_nt(k_ref[pl.ds(pl.multiple_of(blk * tq, tq), tq), hs[h]], qq[h])

    def values(h, blk):
        return _dot(vt_ref[hs[h], pl.ds(pl.multiple_of(blk * tq, tq), tq)], p_scr[h])

    def advance(h, j, stats, mask):
        m, l, a_prev = stats
        acc_scr[h] = a_prev * acc_scr[h] + values(h, jnp.maximum(j - 1, 0))
        s = s_scr[h]
        if mask is not None:
            s = jnp.where(mask, s, NEG_INF)
        m, l, alpha, p = _softmax_update(s, m, l)
        p_scr[h] = p
        return m, l, alpha

    def trip(j, stats):
        out = []
        for h in range(nh):
            out.append(advance(h, j, stats[h], None))
            s_scr[h] = scores(h, j + 1)
        return tuple(out)

    for h in range(nh):
        s_scr[h] = scores(h, 0)
        p_scr[h] = jnp.zeros((tq, 2 * tq), BF16)
        acc_scr[h] = jnp.zeros((DA_VD, 2 * tq), F32)
    init = (jnp.full((1, 2 * tq), NEG_INF, F32), jnp.zeros((1, 2 * tq), F32),
            jnp.ones((1, 2 * tq), F32))
    stats = lax.fori_loop(0, i, trip, (init,) * nh)

    key = lax.broadcasted_iota(jnp.int32, (tq, 2 * tq), 0)
    qry = lax.broadcasted_iota(jnp.int32, (tq, 2 * tq), 1)
    qry = jnp.where(qry >= tq, qry - tq, qry)
    mask = (key // CHUNK) <= (qry // CHUNK)
    lam = _lambda_of(lam_ref, lam_init)
    for h in range(nh):
        _, l, alpha = advance(h, i, stats[h], mask)
        acc = alpha * acc_scr[h] + values(h, i)
        o_t = acc[:, :tq] / l[:, :tq] - lam * (acc[:, tq:] / l[:, tq:])
        o_ref[:, hs[h]] = _rms(o_t.T, subg_ref[...]) * (1.0 - lam_init) * siga_ref[:, hs[h]]


def _attn_prompt(lam4, q_bf, k_bf, vt_bf, siga, subg, B, L, tq, lam_init, nh=4):
    nq = L // tq
    qmap = lambda b, h, i: (b * nq + i, h)
    fixed = lambda b, h, i: (0, 0)
    return pl.pallas_call(
        functools.partial(_attn_prompt_kernel, tq=tq, nh=nh, lam_init=lam_init),
        grid=(B, DA_HEADS // nh, nq),
        in_specs=[pl.BlockSpec((4, DA_HD), fixed),
                  pl.BlockSpec((tq, nh * LANES), qmap),
                  pl.BlockSpec((L, nh * LANES), lambda b, h, i: (b, h)),
                  pl.BlockSpec((nh * DA_VD, L), lambda b, h, i: (h, b)),
                  pl.BlockSpec((tq, nh * LANES), qmap),
                  pl.BlockSpec((1, DA_VD), fixed)],
        out_specs=pl.BlockSpec((tq, nh * LANES), qmap),
        out_shape=jax.ShapeDtypeStruct((B * L, D_MODEL), F32),
        scratch_shapes=[pltpu.VMEM((nh, tq, 2 * tq), F32),
                        pltpu.VMEM((nh, tq, 2 * tq), BF16),
                        pltpu.VMEM((nh, DA_VD, 2 * tq), F32)],
        compiler_params=_params("parallel", "parallel", "arbitrary"),
        name="attn_prompt",
    )(lam4, q_bf, k_bf, vt_bf, siga, subg)


def _attn_sample_kernel(lam_ref, q_ref, pk_ref, pv_ref, k_ref, v_ref, siga_ref, subg_ref, o_ref,
                        *, tq, lam_init):
    qq = _stack_maps(q_ref[...])
    init = (jnp.full((2 * tq, 1), NEG_INF, F32), jnp.zeros((2 * tq, 1), F32),
            jnp.zeros((2 * tq, DA_VD), F32))
    carry = _softmax_step(qq, pk_ref[...].astype(BF16), pv_ref[...].astype(BF16), init, None)
    carry = _softmax_step(qq, k_ref[...], v_ref[...], carry, None)
    o_ref[...] = _da_finish(carry, tq, _lambda_of(lam_ref, lam_init), lam_init,
                            subg_ref[...], siga_ref[...])


def _attn_sample(lam4, q_bf, past_k, past_v, k_bf, v_bf, siga, subg, B, L, lam_init):
    P = past_k.shape[1]
    qmap = lambda b, h: (b, h)
    pmap = lambda b, h: (b, 0, h)
    fixed = lambda b, h: (0, 0)
    return pl.pallas_call(
        functools.partial(_attn_sample_kernel, tq=L, lam_init=lam_init),
        grid=(B, DA_HEADS),
        in_specs=[pl.BlockSpec((4, DA_HD), fixed),
                  pl.BlockSpec((L, LANES), qmap),
                  pl.BlockSpec((None, P, LANES), pmap),
                  pl.BlockSpec((None, P, LANES), pmap),
                  pl.BlockSpec((L, LANES), qmap),
                  pl.BlockSpec((L, LANES), qmap),
                  pl.BlockSpec((L, LANES), qmap),
                  pl.BlockSpec((1, DA_VD), fixed)],
        out_specs=pl.BlockSpec((L, LANES), qmap),
        out_shape=jax.ShapeDtypeStruct((B * L, D_MODEL), F32),
        compiler_params=_params("parallel", "parallel"),
        name="attn_sample",
    )(lam4, q_bf, past_k, past_v, k_bf, v_bf, siga, subg)


def _gla_kernel(x_ref, g1_ref, wg_ref, wlr_ref, wa_ref, ba_ref, glag_ref, s0_ref, oda_ref,
                merged_ref, sfin_ref, st_scr, *, rows, chunk):
    t = pl.program_id(1)

    @pl.when(t == 0)
    def _():
        for hh in range(GLA_HEADS):
            st_scr[hh] = s0_ref[hh].T

    h = _rms(x_ref[...], g1_ref[...]).astype(BF16)
    proj = _dot(h, wg_ref[...])
    gq = proj[:, 0:GLA_K]
    gk = proj[:, GLA_K:2 * GLA_K]
    gv = proj[:, 2 * GLA_K:2 * GLA_K + GLA_V]
    gr = proj[:, 2 * GLA_K + GLA_V:2 * GLA_K + 2 * GLA_V]
    gb = proj[:, 2 * GLA_K + 2 * GLA_V:]
    g_lr = _dot(h, wlr_ref[...])
    z = _dot(g_lr.astype(BF16), wa_ref[...]) + ba_ref[...]
    log_a = (jnp.minimum(z, 0.0) - jnp.log1p(jnp.exp(-jnp.abs(z)))) / GLA_GATE_TAU

    ri = lax.broadcasted_iota(jnp.int32, (chunk, chunk), 0)
    ci = lax.broadcasted_iota(jnp.int32, (chunk, chunk), 1)
    causal = ci <= ri
    tri = causal.astype(F32)

    outs = []
    for c in range(rows // chunk):
        rs = slice(c * chunk, (c + 1) * chunk)
        b = jnp.dot(tri, log_a[rs], precision=lax.Precision.HIGHEST, preferred_element_type=F32)
        b_last = b[chunk - 1:chunk, :]
        e_b = jnp.exp(b)
        e_nb = jnp.exp(-b)
        e_kl = jnp.exp(b_last - b)
        e_last = jnp.exp(b_last)
        heads = []
        for hh in range(GLA_HEADS):
            ks = slice(hh * GLA_HK, (hh + 1) * GLA_HK)
            vs = slice(hh * GLA_HV, (hh + 1) * GLA_HV)
            qd = (gq[rs, ks] * (GLA_HK ** -0.5) * e_b[:, ks]).astype(BF16)
            kd = (gk[rs, ks] * e_nb[:, ks]).astype(BF16)
            kl = (gk[rs, ks] * e_kl[:, ks]).astype(BF16)
            vv = gv[rs, vs].astype(BF16)
            att = jnp.where(causal, _dot_nt(qd, kd), 0.0)
            st = st_scr[hh]
            heads.append(_dot(att.astype(BF16), vv) + _dot_nt(qd, st.astype(BF16)))
            st_scr[hh] = st * e_last[:, ks] + _dot_tn(vv, kl)
        outs.append(jnp.concatenate(heads, axis=1))
    o = jnp.concatenate(outs, axis=0) if len(outs) > 1 else outs[0]

    normed = []
    for hh in range(GLA_HEADS):
        vs = slice(hh * GLA_HV, (hh + 1) * GLA_HV)
        normed.append(_rms(o[:, vs], glag_ref[...]))
    o = jnp.concatenate(normed, axis=1) * (gr * jax.nn.sigmoid(gr))
    merged_ref[...] = (oda_ref[...] + jax.nn.sigmoid(gb) * o).astype(BF16)

    @pl.when(t == pl.num_programs(1) - 1)
    def _():
        for hh in range(GLA_HEADS):
            sfin_ref[hh] = st_scr[hh].T


def _gla(x2d, g1, w_g, w_lr, w_a, b_a, gla_g, s0, oda, B, L, rows, chunk):
    nt = L // rows
    rmap = lambda b, t: (b * nt + t, 0)
    fixed = lambda b, t: (0, 0)
    smap = lambda b, t: (b, 0, 0, 0)
    return pl.pallas_call(
        functools.partial(_gla_kernel, rows=rows, chunk=chunk),
        grid=(B, nt),
        in_specs=[pl.BlockSpec((rows, D_MODEL), rmap),
                  pl.BlockSpec((1, D_MODEL), fixed),
                  pl.BlockSpec(w_g.shape, fixed),
                  pl.BlockSpec(w_lr.shape, fixed),
                  pl.BlockSpec(w_a.shape, fixed),
                  pl.BlockSpec((1, GLA_K), fixed),
                  pl.BlockSpec((1, GLA_HV), fixed),
                  pl.BlockSpec((None, GLA_HEADS, GLA_HK, GLA_HV), smap),
                  pl.BlockSpec((rows, D_MODEL), rmap)],
        out_specs=[pl.BlockSpec((rows, D_MODEL), rmap),
                   pl.BlockSpec((None, GLA_HEADS, GLA_HK, GLA_HV), smap)],
        out_shape=[jax.ShapeDtypeStruct((B * L, D_MODEL), BF16),
                   jax.ShapeDtypeStruct((B, GLA_HEADS, GLA_HK, GLA_HV), F32)],
        scratch_shapes=[pltpu.VMEM((GLA_HEADS, GLA_HV, GLA_HK), F32)],
        compiler_params=_params("parallel", "arbitrary"),
        name="gla",
    )(x2d, g1, w_g, w_lr, w_a, b_a, gla_g, s0, oda)


def _peer_query_kernel(x_ref, m_ref, wo_ref, g2_ref, wq_ref, sk_ref, x1_ref, h2t_ref, st_ref):
    x1 = x_ref[...] + _dot(m_ref[...], wo_ref[...])
    x1_ref[...] = x1
    h2f = _rms(x1, g2_ref[...])
    h2t_ref[...] = h2f.T.astype(BF16)
    q = _dot(h2f.astype(BF16), wq_ref[...]).astype(BF16)
    for j in range(2 * PEER_HEADS):
        s = _dot_nt(sk_ref[j], q[:, j * PK_DIM:(j + 1) * PK_DIM])
        for strip in range(s.shape[1] // LANES):
            st_ref[j, strip] = s[:, strip * LANES:(strip + 1) * LANES]


def _peer_query(x2d, merged, w_out, g2, wq, subkeys, tm):
    T = x2d.shape[0]
    row = lambda i: (i, 0)
    fixed = lambda i: (0, 0)
    nj = 2 * PEER_HEADS
    return pl.pallas_call(
        _peer_query_kernel,
        grid=(T // tm,),
        in_specs=[pl.BlockSpec((tm, D_MODEL), row),
                  pl.BlockSpec((tm, D_MODEL), row),
                  pl.BlockSpec((D_MODEL, D_MODEL), fixed),
                  pl.BlockSpec((1, D_MODEL), fixed),
                  pl.BlockSpec((D_MODEL, nj * PK_DIM), fixed),
                  pl.BlockSpec((nj, N_KEYS, PK_DIM), lambda i: (0, 0, 0))],
        out_specs=[pl.BlockSpec((tm, D_MODEL), row),
                   pl.BlockSpec((D_MODEL, tm), lambda i: (0, i)),
                   pl.BlockSpec((nj, tm // LANES, N_KEYS, LANES), lambda i: (0, i, 0, 0))],
        out_shape=[jax.ShapeDtypeStruct((T, D_MODEL), F32),
                   jax.ShapeDtypeStruct((D_MODEL, T), BF16),
                   jax.ShapeDtypeStruct((nj, T // LANES, N_KEYS, LANES), F32)],
        compiler_params=_params("parallel"),
        name="peer_query",
    )(x2d, merged, w_out, g2, wq, subkeys)


def _top_sorted(s, k):
    work = s
    rows = []
    for _ in range(k):
        m = jnp.max(work, axis=0, keepdims=True)
        rows.append(m)
        work = jnp.where(work == m, NEG_INF, work)
    return jnp.concatenate(rows, axis=0)


def _peer_route_kernel(st_ref, e1_ref, e2_ref, th_ref):
    for strip in range(st_ref.shape[1]):
        e1, e2, theta = _route_strip(st_ref[0, strip], st_ref[1, strip])
        e1_ref[strip] = e1
        e2_ref[strip] = e2
        th_ref[:, strip * LANES:(strip + 1) * LANES] = theta


def _route_strip(s1, s2):
    a = _top_sorted(s1, PEER_TOPK)
    b = _top_sorted(s2, PEER_TOPK)
    half = PEER_TOPK // 2
    cands = [a[0:half] + b[0:1], a[half:] + b[0:1], a[0:1] + b[half:]]
    for c in range(1, half):
        cands.append(a[0:half] + b[c:c + 1])
    cand = jnp.concatenate(cands, axis=0)
    work = cand
    for _ in range(PEER_TOPK):
        theta = jnp.max(work, axis=0, keepdims=True)
        work = jnp.where(work == theta, NEG_INF, work)
    top = a[0:1] + b[0:1]
    zsum = jnp.sum(jnp.where(cand >= theta, jnp.exp(cand - top), 0.0), axis=0, keepdims=True)
    return jnp.exp(s1 - a[0:1]), jnp.exp(s2 - b[0:1]) / zsum, theta


def _peer_route(st, tt):
    ns_all = st.shape[2]
    ns = tt // LANES
    tile = pl.BlockSpec((None, ns, N_KEYS, LANES), lambda h, i: (h, i, 0, 0))
    return pl.pallas_call(
        _peer_route_kernel,
        grid=(PEER_HEADS, ns_all // ns),
        in_specs=[pl.BlockSpec((None, 2, ns, N_KEYS, LANES), lambda h, i: (h, 0, i, 0, 0))],
        out_specs=[tile, tile, pl.BlockSpec((None, 1, tt), lambda h, i: (h, 0, i))],
        out_shape=[jax.ShapeDtypeStruct((PEER_HEADS, ns_all, N_KEYS, LANES), F32),
                   jax.ShapeDtypeStruct((PEER_HEADS, ns_all, N_KEYS, LANES), F32),
                   jax.ShapeDtypeStruct((PEER_HEADS, 1, ns_all * LANES), F32)],
        compiler_params=_params("parallel", "parallel"),
        name="peer_route",
    )(st)


def _pack_tables_kernel(u_ref, v_ref, upk_ref, vtpk_ref):
    upk_ref[...] = pltpu.bitcast(u_ref[...].astype(BF16), jnp.uint32)
    vtpk_ref[...] = pltpu.bitcast(v_ref[...].T.astype(BF16), jnp.uint32)


def _pack_tables(u, v, rows):
    n = u.shape[0]
    return pl.pallas_call(
        _pack_tables_kernel,
        grid=(n // rows,),
        in_specs=[pl.BlockSpec((rows, D_MODEL), lambda i: (i, 0)),
                  pl.BlockSpec((rows, D_MODEL), lambda i: (i, 0))],
        out_specs=[pl.BlockSpec((rows // 2, D_MODEL), lambda i: (i, 0)),
                   pl.BlockSpec((D_MODEL // 2, rows), lambda i: (0, i))],
        out_shape=[jax.ShapeDtypeStruct((n // 2, D_MODEL), jnp.uint32),
                   jax.ShapeDtypeStruct((D_MODEL // 2, n), jnp.uint32)],
        compiler_params=_params("parallel"),
        name="pack_tables",
    )(u, v)


def _peer_dense_kernel(h2t_ref, x1_ref, s1_ref, s2_ref, e1_ref, e2_ref, th_ref, u_ref, vt_ref, gf_ref,
                       y_ref, a0_scr, a1_scr, p0_scr, p1_scr, acc_scr, *, tt, ni):
    e = pl.program_id(1)
    n_strips = tt // LANES
    a_rows = ni * N_KEYS // n_strips
    d_rows = D_MODEL // n_strips

    @pl.when(e == 0)
    def _():
        for ref in (a0_scr, a1_scr, p0_scr, p1_scr, acc_scr):
            ref[...] = jnp.zeros_like(ref)

    def step(a_new, a_mid, p_old, p_mid):

        def per_strip(ls, carry):
            ar = pl.ds(pl.multiple_of(ls * a_rows, a_rows), a_rows)
            u_rows = pl.ds(pl.multiple_of(ls * (a_rows // 2), a_rows // 2), a_rows // 2)
            scores = _dot(pltpu.bitcast(u_ref[u_rows, :], BF16), h2t_ref[...])
            for s in range(n_strips):
                a_new[s, ar, :] = scores[:, s * LANES:(s + 1) * LANES]
            dr = pl.ds(pl.multiple_of(ls * d_rows, d_rows), d_rows)
            v_rows = pl.ds(pl.multiple_of(ls * (d_rows // 2), d_rows // 2), d_rows // 2)
            weighted = jnp.concatenate([p_old[s] for s in range(n_strips)], axis=1)
            acc_scr[dr, :] += _dot(pltpu.bitcast(vt_ref[v_rows, :], BF16), weighted)
            cols = pl.ds(pl.multiple_of(ls * LANES, LANES), LANES)
            for il in range(ni):
                rows = slice(il * N_KEYS, (il + 1) * N_KEYS)
                w = jnp.zeros((N_KEYS, LANES), F32)
                for hh in range(PEER_HEADS):
                    s1row = s1_ref[hh, ls, il:il + 1, :]
                    e1row = e1_ref[hh, ls, il:il + 1, :]
                    sel = (s1row + s2_ref[hh, ls]) >= th_ref[hh, :, cols]
                    w = w + jnp.where(sel, e2_ref[hh, ls], 0.0) * e1row
                a = a_mid[ls, rows, :]
                act = 0.5 * a * (1.0 + lax.erf(a * (2.0 ** -0.5)))
                p_mid[ls, rows, :] = (w * act).astype(BF16)
            return carry

        lax.fori_loop(0, n_strips, per_strip, 0)

    @pl.when(e % 2 == 0)
    def _():
        step(a0_scr, a1_scr, p0_scr, p1_scr)

    @pl.when(e % 2 == 1)
    def _():
        step(a1_scr, a0_scr, p1_scr, p0_scr)

    @pl.when(e == pl.num_programs(1) - 1)
    def _():
        y_ref[...] = _rms(x1_ref[...] + acc_scr[...].T, gf_ref[...])


def _peer_dense(h2t, x1, st, e1, e2, th, u_bf, vt_bf, gf, tt, ni):
    T = x1.shape[0]
    ne = N_KEYS // ni
    et = ni * N_KEYS
    ns = tt // LANES
    tok = lambda t, e: (t, 0)
    tile = lambda e, lag: jnp.clip(e - lag, 0, ne - 1)
    return pl.pallas_call(
        functools.partial(_peer_dense_kernel, tt=tt, ni=ni),
        grid=(T // tt, ne + 2),
        in_specs=[pl.BlockSpec((D_MODEL, tt), lambda t, e: (0, t)),
                  pl.BlockSpec((tt, D_MODEL), tok),
                  pl.BlockSpec((PEER_HEADS, None, ns, ni, LANES),
                               lambda t, e: (0, 0, t, tile(e, 1), 0)),
                  pl.BlockSpec((PEER_HEADS, None, ns, N_KEYS, LANES), lambda t, e: (0, 1, t, 0, 0)),
                  pl.BlockSpec((PEER_HEADS, ns, ni, LANES), lambda t, e: (0, t, tile(e, 1), 0)),
                  pl.BlockSpec((PEER_HEADS, ns, N_KEYS, LANES), lambda t, e: (0, t, 0, 0)),
                  pl.BlockSpec((PEER_HEADS, 1, tt), lambda t, e: (0, 0, t)),
                  pl.BlockSpec((et // 2, D_MODEL), lambda t, e: (tile(e, 0), 0)),
                  pl.BlockSpec((D_MODEL // 2, et), lambda t, e: (0, tile(e, 2))),
                  pl.BlockSpec((1, D_MODEL), lambda t, e: (0, 0))],
        out_specs=pl.BlockSpec((tt, D_MODEL), tok),
        out_shape=jax.ShapeDtypeStruct((T, D_MODEL), F32),
        scratch_shapes=[pltpu.VMEM((ns, et, LANES), F32),
                        pltpu.VMEM((ns, et, LANES), F32),
                        pltpu.VMEM((ns, et, LANES), BF16),
                        pltpu.VMEM((ns, et, LANES), BF16),
                        pltpu.VMEM((D_MODEL, tt), F32)],
        compiler_params=_params("parallel", "arbitrary"),
        name="peer_dense",
    )(h2t, x1, st, st, e1, e2, th, u_bf, vt_bf, gf)


def _rope_tables(pos):
    half = ROT_DIM // 2
    inv = jnp.power(ROPE_THETA, -jnp.arange(half, dtype=F32) * 2.0 / ROT_DIM)
    ang = pos.astype(F32)[:, None] * inv[None, :]
    cos, sin = jnp.cos(ang), jnp.sin(ang)
    n = pos.shape[0]
    pad = jnp.zeros((n, DA_HD - ROT_DIM), F32)
    zero = jnp.zeros((n, half), F32)
    cos_t = jnp.concatenate([cos, cos, pad + 1.0], axis=1)
    sin_a = jnp.concatenate([-sin, zero, pad], axis=1)
    sin_b = jnp.concatenate([zero, sin, pad], axis=1)
    return tuple(jnp.tile(t, (1, LANES // DA_HD)) for t in (cos_t, sin_a, sin_b))


def _trunk(x, pos, past_k, past_v, s0, chunk, wts, layer, tm, tq, gla_rows, tt, ni):
    B, L, _ = x.shape
    T = B * L
    lam_init = 0.8 - 0.6 * math.exp(-0.3 * layer)
    x2d = x.reshape(T, D_MODEL)
    cos_t, sin_a, sin_b = _rope_tables(pos)
    if L < tm:
        cos_t, sin_a, sin_b = (jnp.tile(t, (tm // L, 1)) for t in (cos_t, sin_a, sin_b))

    q_bf, k32, k_bf, v32, v_bf, siga = _da_proj(
        x2d, wts["g1"], wts["w_qkv"], wts["w_ga"], cos_t, sin_a, sin_b, tm,
        transpose_v=past_k is None)
    if past_k is None:
        oda = _attn_prompt(wts["lam4"], q_bf, k_bf, v_bf, siga, wts["subg"], B, L, tq, lam_init)
    else:
        oda = _attn_sample(wts["lam4"], q_bf, past_k, past_v, k_bf, v_bf, siga, wts["subg"],
                           B, L, lam_init)
    merged, s_fin = _gla(x2d, wts["g1"], wts["w_g"], wts["w_lr"], wts["w_a"], wts["b_a"],
                         wts["gla_g"], s0, oda, B, L, gla_rows, chunk)
    x1, h2t, st = _peer_query(x2d, merged, wts["w_out"], wts["g2"], wts["wq"], wts["subkeys"], tm)
    st = st.reshape(PEER_HEADS, 2, T // LANES, N_KEYS, LANES)
    e1, e2, th = _peer_route(st, tt)
    y = _peer_dense(h2t, x1, st, e1, e2, th, wts["u"], wts["vt"], wts["gf"], tt, ni)
    return (y.reshape(B, L, D_MODEL), k32.reshape(1, B, L, DA_HEADS, 2, DA_HD),
            v32.reshape(1, B, L, DA_HEADS, DA_VD), s_fin[None])


def kernel(x_prompt, x_sample, cache_da_k, cache_da_v, state_gla, norm1_g, w_in, da_lambda_q1,
           da_lambda_k1, da_lambda_q2, da_lambda_k2, da_subln_g, gla_w_alpha, gla_b_alpha,
           gla_norm_g, w_out, norm2_g, peer_wq, peer_subkeys, peer_u, peer_v, final_norm_g):
    layer = 0
    B, L, _ = x_prompt.shape
    SB, SL, _ = x_sample.shape
    P = cache_da_k.shape[2]

    o_dq, o_gq = 0, 3 * D_MODEL
    o_gr = o_gq + 2 * GLA_K + GLA_V
    o_lr = o_gr + GLA_V
    o_ga = o_lr + GLA_GATE_RANK
    o_gb = o_ga + D_MODEL
    w = w_in[layer]
    pad_rank = LANES - GLA_GATE_RANK
    wts = {
        "g1": norm1_g[layer][None, :],
        "w_qkv": w[:, o_dq:o_gq].astype(BF16),
        "w_ga": w[:, o_ga:o_gb].astype(BF16),
        "w_g": jnp.concatenate([w[:, o_gq:o_lr], w[:, o_gb:]], axis=1).astype(BF16),
        "w_lr": jnp.pad(w[:, o_lr:o_ga], ((0, 0), (0, pad_rank))).astype(BF16),
        "w_a": jnp.pad(gla_w_alpha[layer], ((0, pad_rank), (0, 0))).astype(BF16),
        "b_a": gla_b_alpha[layer][None, :],
        "gla_g": gla_norm_g[layer][None, :],
        "lam4": jnp.stack([da_lambda_q1[layer], da_lambda_k1[layer],
                           da_lambda_q2[layer], da_lambda_k2[layer]]),
        "subg": da_subln_g[layer][None, :],
        "w_out": w_out[layer].astype(BF16),
        "g2": norm2_g[layer][None, :],
        "wq": peer_wq[layer].astype(BF16),
        "subkeys": peer_subkeys[layer].reshape(2 * PEER_HEADS, N_KEYS, PK_DIM).astype(BF16),
        "gf": final_norm_g[None, :],
    }
    wts["u"], wts["vt"] = _pack_tables(peer_u[layer], peer_v[layer], rows=512)

    pos_p = jnp.arange(L, dtype=F32)
    pos_s = P + jnp.arange(SL, dtype=F32)
    zeros_state = jnp.zeros((B, GLA_HEADS, GLA_HK, GLA_HV), F32)
    tm_p = min(256, B * L)
    y_p, k_p, v_p, s_p = _trunk(x_prompt, pos_p, None, None, zeros_state, CHUNK, wts, layer,
                                tm=tm_p, tq=min(256, L), gla_rows=min(256, L),
                                tt=min(512, B * L), ni=8)
    past_k = cache_da_k[layer].reshape(SB, P, D_MODEL)
    past_v = cache_da_v[layer].reshape(SB, P, D_MODEL)
    y_s, k_s, v_s, s_s = _trunk(x_sample, pos_s, past_k, past_v, state_gla[layer], SL, wts, layer,
                                tm=SB * SL, tq=SL, gla_rows=SL, tt=SB * SL, ni=8)
    return (y_p, y_s, k_p, v_p, s_p, k_s, v_s, s_s)
```

```python
import functools
import math

import jax
import jax.numpy as jnp
from jax import lax
from jax.experimental import pallas as pl
from jax.experimental.pallas import tpu as pltpu

F32 = jnp.float32
BF16 = jnp.bfloat16

D_MODEL = 1024
CHUNK = 64
NORM_EPS = 1e-6

DA_HEADS = 8
DA_HD = 64
DA_VD = 128
ROT_DIM = DA_HD // 4
ROPE_THETA = 500000.0
Q_SCALE = DA_HD ** -0.5 * math.log2(math.e)

GLA_HEADS = 4
GLA_K = 512
GLA_V = 1024
GLA_HK = 128
GLA_HV = 256
GLA_GATE_RANK = 16
GLA_GATE_TAU = 16.0

PEER_HEADS = 8
N_KEYS = 128
PK_DIM = 128
PEER_TOPK = 16

LANES = 128
VMEM_LIMIT = 56 * 1024 * 1024

NEG_INF = float("-inf")


def _params(*sem):
    return pltpu.CompilerParams(dimension_semantics=sem, vmem_limit_bytes=VMEM_LIMIT)


def _rms(xf, g):
    ms = jnp.mean(xf * xf, axis=-1, keepdims=True)
    return xf * lax.rsqrt(ms + NORM_EPS) * g


def _dot(a, b):
    return jnp.dot(a, b, preferred_element_type=F32)


def _dot_nt(a, b):
    return lax.dot_general(a, b, (((1,), (1,)), ((), ())), preferred_element_type=F32)


def _dot_tn(a, b):
    return lax.dot_general(a, b, (((0,), (0,)), ((), ())), preferred_element_type=F32)


def _da_proj_kernel(x_ref, g_ref, w_ref, wga_ref, cos_ref, sa_ref, sb_ref,
                    q_ref, k32_ref, kbf_ref, v32_ref, vbf_ref, siga_ref, *, transpose_v):
    h = _rms(x_ref[...], g_ref[...]).astype(BF16)
    cos_t = cos_ref[...]
    sin_a = sa_ref[...]
    sin_b = sb_ref[...]

    def rope(blk):
        return (blk * cos_t + pltpu.roll(blk, LANES - ROT_DIM // 2, 1) * sin_a
                + pltpu.roll(blk, ROT_DIM // 2, 1) * sin_b)

    qf = _dot(h, w_ref[:, 0:D_MODEL])
    for c in range(D_MODEL // LANES):
        sl = slice(c * LANES, (c + 1) * LANES)
        q_ref[:, sl] = (rope(qf[:, sl]) * Q_SCALE).astype(BF16)
    kf = _dot(h, w_ref[:, D_MODEL:2 * D_MODEL])
    for c in range(D_MODEL // LANES):
        sl = slice(c * LANES, (c + 1) * LANES)
        kr = rope(kf[:, sl])
        k32_ref[:, sl] = kr
        kbf_ref[:, sl] = kr.astype(BF16)
    vf = _dot(h, w_ref[:, 2 * D_MODEL:3 * D_MODEL])
    v32_ref[...] = vf
    vbf_ref[...] = (vf.T if transpose_v else vf).astype(BF16)
    siga_ref[...] = jax.nn.sigmoid(_dot(h, wga_ref[...]))


def _da_proj(x2d, g1, w_qkv, w_ga, cos_t, sin_a, sin_b, tm, transpose_v):
    T = x2d.shape[0]
    n_pos_blocks = cos_t.shape[0] // tm
    row = lambda i: (i, 0)
    fixed = lambda i: (0, 0)
    pos = lambda i: (i % n_pos_blocks, 0)
    full = pl.BlockSpec((tm, D_MODEL), row)
    tab = pl.BlockSpec((tm, LANES), pos)
    if transpose_v:
        v_spec = pl.BlockSpec((D_MODEL, tm), lambda i: (0, i))
        v_shape = jax.ShapeDtypeStruct((D_MODEL, T), BF16)
    else:
        v_spec, v_shape = full, jax.ShapeDtypeStruct((T, D_MODEL), BF16)
    return pl.pallas_call(
        functools.partial(_da_proj_kernel, transpose_v=transpose_v),
        grid=(T // tm,),
        in_specs=[full, pl.BlockSpec((1, D_MODEL), fixed),
                  pl.BlockSpec((D_MODEL, 3 * D_MODEL), fixed),
                  pl.BlockSpec((D_MODEL, D_MODEL), fixed), tab, tab, tab],
        out_specs=[full, full, full, full, v_spec, full],
        out_shape=[jax.ShapeDtypeStruct((T, D_MODEL), BF16),
                   jax.ShapeDtypeStruct((T, D_MODEL), F32),
                   jax.ShapeDtypeStruct((T, D_MODEL), BF16),
                   jax.ShapeDtypeStruct((T, D_MODEL), F32),
                   v_shape,
                   jax.ShapeDtypeStruct((T, D_MODEL), F32)],
        compiler_params=_params("parallel"),
        name="da_proj",
    )(x2d, g1, w_qkv, w_ga, cos_t, sin_a, sin_b)


def _lambda_of(lam_ref, lam_init):
    a = lam_ref[...]
    d1 = jnp.sum(a[0:1, :] * a[1:2, :], axis=-1, keepdims=True)
    d2 = jnp.sum(a[2:3, :] * a[3:4, :], axis=-1, keepdims=True)
    return jnp.exp(d1) - jnp.exp(d2) + lam_init


def _stack_maps(q):
    lane = lax.broadcasted_iota(jnp.int32, q.shape, 1)
    zero = jnp.zeros_like(q)
    return jnp.concatenate([jnp.where(lane < DA_HD, q, zero),
                            jnp.where(lane >= DA_HD, q, zero)], axis=0)


def _softmax_step(qq, kb, vb, carry, mask):
    m, l, acc = carry
    s = _dot_nt(qq, kb)
    if mask is not None:
        s = jnp.where(mask, s, NEG_INF)
    m_new = jnp.maximum(m, jnp.max(s, axis=1, keepdims=True))
    alpha = jnp.exp2(m - m_new)
    p = jnp.exp2(s - m_new)
    l = alpha * l + jnp.sum(p, axis=1, keepdims=True)
    acc = alpha * acc + _dot(p.astype(BF16), vb)
    return m_new, l, acc


def _da_finish(carry, tq, lam, lam_init, subg, siga):
    _, l, acc = carry
    o = acc[:tq] / l[:tq] - lam * (acc[tq:] / l[tq:])
    return _rms(o, subg) * (1.0 - lam_init) * siga


def _softmax_update(s, m, l):
    m_new = jnp.maximum(m, jnp.max(s, axis=0, keepdims=True))
    alpha = jnp.exp2(m - m_new)
    p = jnp.exp2(s - m_new)
    return m_new, alpha * l + jnp.sum(p, axis=0, keepdims=True), alpha, p.astype(BF16)


def _attn_prompt_kernel(lam_ref, q_ref, k_ref, vt_ref, siga_ref, subg_ref, o_ref,
                        s_scr, p_scr, acc_scr, *, tq, nh, lam_init):
    i = pl.program_id(2)
    hs = [slice(h * LANES, (h + 1) * LANES) for h in range(nh)]
    qq = [_stack_maps(q_ref[:, s]) for s in hs]

    def scores(h, blk):
        return _dot_nt(k_ref[pl.ds(pl.multiple_of(blk * tq, tq), tq), hs[h]], qq[h])

    def values(h, blk):
        return _dot(vt_ref[hs[h], pl.ds(pl.multiple_of(blk * tq, tq), tq)], p_scr[h])

    def advance(h, j, stats, mask):
        m, l, a_prev = stats
        acc_scr[h] = a_prev * acc_scr[h] + values(h, jnp.maximum(j - 1, 0))
        s = s_scr[h]
        if mask is not None:
            s = jnp.where(mask, s, NEG_INF)
        m, l, alpha, p = _softmax_update(s, m, l)
        p_scr[h] = p
        return m, l, alpha

    def trip(j, stats):
        out = []
        for h in range(nh):
            out.append(advance(h, j, stats[h], None))
            s_scr[h] = scores(h, j + 1)
        return tuple(out)

    for h in range(nh):
        s_scr[h] = scores(h, 0)
        p_scr[h] = jnp.zeros((tq, 2 * tq), BF16)
        acc_scr[h] = jnp.zeros((DA_VD, 2 * tq), F32)
    init = (jnp.full((1, 2 * tq), NEG_INF, F32), jnp.zeros((1, 2 * tq), F32),
            jnp.ones((1, 2 * tq), F32))
    stats = lax.fori_loop(0, i, trip, (init,) * nh)

    key = lax.broadcasted_iota(jnp.int32, (tq, 2 * tq), 0)
    qry = lax.broadcasted_iota(jnp.int32, (tq, 2 * tq), 1)
    qry = jnp.where(qry >= tq, qry - tq, qry)
    mask = (key // CHUNK) <= (qry // CHUNK)
    lam = _lambda_of(lam_ref, lam_init)
    for h in range(nh):
        _, l, alpha = advance(h, i, stats[h], mask)
        acc = alpha * acc_scr[h] + values(h, i)
        o_t = acc[:, :tq] / l[:, :tq] - lam * (acc[:, tq:] / l[:, tq:])
        o_ref[:, hs[h]] = _rms(o_t.T, subg_ref[...]) * (1.0 - lam_init) * siga_ref[:, hs[h]]


def _attn_prompt(lam4, q_bf, k_bf, vt_bf, siga, subg, B, L, tq, lam_init, nh=4):
    nq = L // tq
    qmap = lambda b, h, i: (b * nq + i, h)
    fixed = lambda b, h, i: (0, 0)
    return pl.pallas_call(
        functools.partial(_attn_prompt_kernel, tq=tq, nh=nh, lam_init=lam_init),
        grid=(B, DA_HEADS // nh, nq),
        in_specs=[pl.BlockSpec((4, DA_HD), fixed),
                  pl.BlockSpec((tq, nh * LANES), qmap),
                  pl.BlockSpec((L, nh * LANES), lambda b, h, i: (b, h)),
                  pl.BlockSpec((nh * DA_VD, L), lambda b, h, i: (h, b)),
                  pl.BlockSpec((tq, nh * LANES), qmap),
                  pl.BlockSpec((1, DA_VD), fixed)],
        out_specs=pl.BlockSpec((tq, nh * LANES), qmap),
        out_shape=jax.ShapeDtypeStruct((B * L, D_MODEL), F32),
        scratch_shapes=[pltpu.VMEM((nh, tq, 2 * tq), F32),
                        pltpu.VMEM((nh, tq, 2 * tq), BF16),
                        pltpu.VMEM((nh, DA_VD, 2 * tq), F32)],
        compiler_params=_params("parallel", "parallel", "arbitrary"),
        name="attn_prompt",
    )(lam4, q_bf, k_bf, vt_bf, siga, subg)


def _attn_sample_kernel(lam_ref, q_ref, pk_ref, pv_ref, k_ref, v_ref, siga_ref, subg_ref, o_ref,
                        *, tq, lam_init):
    qq = _stack_maps(q_ref[...])
    init = (jnp.full((2 * tq, 1), NEG_INF, F32), jnp.zeros((2 * tq, 1), F32),
            jnp.zeros((2 * tq, DA_VD), F32))
    carry = _softmax_step(qq, pk_ref[...].astype(BF16), pv_ref[...].astype(BF16), init, None)
    carry = _softmax_step(qq, k_ref[...], v_ref[...], carry, None)
    o_ref[...] = _da_finish(carry, tq, _lambda_of(lam_ref, lam_init), lam_init,
                            subg_ref[...], siga_ref[...])


def _attn_sample(lam4, q_bf, past_k, past_v, k_bf, v_bf, siga, subg, B, L, lam_init):
    P = past_k.shape[1]
    qmap = lambda b, h: (b, h)
    pmap = lambda b, h: (b, 0, h)
    fixed = lambda b, h: (0, 0)
    return pl.pallas_call(
        functools.partial(_attn_sample_kernel, tq=L, lam_init=lam_init),
        grid=(B, DA_HEADS),
        in_specs=[pl.BlockSpec((4, DA_HD), fixed),
                  pl.BlockSpec((L, LANES), qmap),
                  pl.BlockSpec((None, P, LANES), pmap),
                  pl.BlockSpec((None, P, LANES), pmap),
                  pl.BlockSpec((L, LANES), qmap),
                  pl.BlockSpec((L, LANES), qmap),
                  pl.BlockSpec((L, LANES), qmap),
                  pl.BlockSpec((1, DA_VD), fixed)],
        out_specs=pl.BlockSpec((L, LANES), qmap),
        out_shape=jax.ShapeDtypeStruct((B * L, D_MODEL), F32),
        compiler_params=_params("parallel", "parallel"),
        name="attn_sample",
    )(lam4, q_bf, past_k, past_v, k_bf, v_bf, siga, subg)


def _gla_kernel(x_ref, g1_ref, wg_ref, wlr_ref, wa_ref, ba_ref, glag_ref, s0_ref, oda_ref,
                merged_ref, sfin_ref, st_scr, *, rows, chunk):
    t = pl.program_id(1)

    @pl.when(t == 0)
    def _():
        for hh in range(GLA_HEADS):
            st_scr[hh] = s0_ref[hh].T

    h = _rms(x_ref[...], g1_ref[...]).astype(BF16)
    proj = _dot(h, wg_ref[...])
    gq = proj[:, 0:GLA_K]
    gk = proj[:, GLA_K:2 * GLA_K]
    gv = proj[:, 2 * GLA_K:2 * GLA_K + GLA_V]
    gr = proj[:, 2 * GLA_K + GLA_V:2 * GLA_K + 2 * GLA_V]
    gb = proj[:, 2 * GLA_K + 2 * GLA_V:]
    g_lr = _dot(h, wlr_ref[...])
    z = _dot(g_lr.astype(BF16), wa_ref[...]) + ba_ref[...]
    log_a = (jnp.minimum(z, 0.0) - jnp.log1p(jnp.exp(-jnp.abs(z)))) / GLA_GATE_TAU

    ri = lax.broadcasted_iota(jnp.int32, (chunk, chunk), 0)
    ci = lax.broadcasted_iota(jnp.int32, (chunk, chunk), 1)
    causal = ci <= ri
    tri = causal.astype(F32)

    outs = []
    for c in range(rows // chunk):
        rs = slice(c * chunk, (c + 1) * chunk)
        b = jnp.dot(tri, log_a[rs], precision=lax.Precision.HIGHEST, preferred_element_type=F32)
        b_last = b[chunk - 1:chunk, :]
        e_b = jnp.exp(b)
        e_nb = jnp.exp(-b)
        e_kl = jnp.exp(b_last - b)
        e_last = jnp.exp(b_last)
        heads = []
        for hh in range(GLA_HEADS):
            ks = slice(hh * GLA_HK, (hh + 1) * GLA_HK)
            vs = slice(hh * GLA_HV, (hh + 1) * GLA_HV)
            qd = (gq[rs, ks] * (GLA_HK ** -0.5) * e_b[:, ks]).astype(BF16)
            kd = (gk[rs, ks] * e_nb[:, ks]).astype(BF16)
            kl = (gk[rs, ks] * e_kl[:, ks]).astype(BF16)
            vv = gv[rs, vs].astype(BF16)
            att = jnp.where(causal, _dot_nt(qd, kd), 0.0)
            st = st_scr[hh]
            heads.append(_dot(att.astype(BF16), vv) + _dot_nt(qd, st.astype(BF16)))
            st_scr[hh] = st * e_last[:, ks] + _dot_tn(vv, kl)
        outs.append(jnp.concatenate(heads, axis=1))
    o = jnp.concatenate(outs, axis=0) if len(outs) > 1 else outs[0]

    normed = []
    for hh in range(GLA_HEADS):
        vs = slice(hh * GLA_HV, (hh + 1) * GLA_HV)
        normed.append(_rms(o[:, vs], glag_ref[...]))
    o = jnp.concatenate(normed, axis=1) * (gr * jax.nn.sigmoid(gr))
    merged_ref[...] = (oda_ref[...] + jax.nn.sigmoid(gb) * o).astype(BF16)

    @pl.when(t == pl.num_programs(1) - 1)
    def _():
        for hh in range(GLA_HEADS):
            sfin_ref[hh] = st_scr[hh].T


def _gla(x2d, g1, w_g, w_lr, w_a, b_a, gla_g, s0, oda, B, L, rows, chunk):
    nt = L // rows
    rmap = lambda b, t: (b * nt + t, 0)
    fixed = lambda b, t: (0, 0)
    smap = lambda b, t: (b, 0, 0, 0)
    return pl.pallas_call(
        functools.partial(_gla_kernel, rows=rows, chunk=chunk),
        grid=(B, nt),
        in_specs=[pl.BlockSpec((rows, D_MODEL), rmap),
                  pl.BlockSpec((1, D_MODEL), fixed),
                  pl.BlockSpec(w_g.shape, fixed),
                  pl.BlockSpec(w_lr.shape, fixed),
                  pl.BlockSpec(w_a.shape, fixed),
                  pl.BlockSpec((1, GLA_K), fixed),
                  pl.BlockSpec((1, GLA_HV), fixed),
                  pl.BlockSpec((None, GLA_HEADS, GLA_HK, GLA_HV), smap),
                  pl.BlockSpec((rows, D_MODEL), rmap)],
        out_specs=[pl.BlockSpec((rows, D_MODEL), rmap),
                   pl.BlockSpec((None, GLA_HEADS, GLA_HK, GLA_HV), smap)],
        out_shape=[jax.ShapeDtypeStruct((B * L, D_MODEL), BF16),
                   jax.ShapeDtypeStruct((B, GLA_HEADS, GLA_HK, GLA_HV), F32)],
        scratch_shapes=[pltpu.VMEM((GLA_HEADS, GLA_HV, GLA_HK), F32)],
        compiler_params=_params("parallel", "arbitrary"),
        name="gla",
    )(x2d, g1, w_g, w_lr, w_a, b_a, gla_g, s0, oda)


def _peer_query_kernel(x_ref, m_ref, wo_ref, g2_ref, wq_ref, sk_ref, x1_ref, h2t_ref, st_ref):
    x1 = x_ref[...] + _dot(m_ref[...], wo_ref[...])
    x1_ref[...] = x1
    h2f = _rms(x1, g2_ref[...])
    h2t_ref[...] = h2f.T.astype(BF16)
    q = _dot(h2f.astype(BF16), wq_ref[...]).astype(BF16)
    for j in range(2 * PEER_HEADS):
        s = _dot_nt(sk_ref[j], q[:, j * PK_DIM:(j + 1) * PK_DIM])
        for strip in range(s.shape[1] // LANES):
            st_ref[j, strip] = s[:, strip * LANES:(strip + 1) * LANES]


def _peer_query(x2d, merged, w_out, g2, wq, subkeys, tm):
    T = x2d.shape[0]
    row = lambda i: (i, 0)
    fixed = lambda i: (0, 0)
    nj = 2 * PEER_HEADS
    return pl.pallas_call(
        _peer_query_kernel,
        grid=(T // tm,),
        in_specs=[pl.BlockSpec((tm, D_MODEL), row),
                  pl.BlockSpec((tm, D_MODEL), row),
                  pl.BlockSpec((D_MODEL, D_MODEL), fixed),
                  pl.BlockSpec((1, D_MODEL), fixed),
                  pl.BlockSpec((D_MODEL, nj * PK_DIM), fixed),
                  pl.BlockSpec((nj, N_KEYS, PK_DIM), lambda i: (0, 0, 0))],
        out_specs=[pl.BlockSpec((tm, D_MODEL), row),
                   pl.BlockSpec((D_MODEL, tm), lambda i: (0, i)),
                   pl.BlockSpec((nj, tm // LANES, N_KEYS, LANES), lambda i: (0, i, 0, 0))],
        out_shape=[jax.ShapeDtypeStruct((T, D_MODEL), F32),
                   jax.ShapeDtypeStruct((D_MODEL, T), BF16),
                   jax.ShapeDtypeStruct((nj, T // LANES, N_KEYS, LANES), F32)],
        compiler_params=_params("parallel"),
        name="peer_query",
    )(x2d, merged, w_out, g2, wq, subkeys)


def _top_sorted(s, k):
    work = s
    rows = []
    for _ in range(k):
        m = jnp.max(work, axis=0, keepdims=True)
        rows.append(m)
        work = jnp.where(work == m, NEG_INF, work)
    return jnp.concatenate(rows, axis=0)


def _peer_route_kernel(st_ref, e1_ref, e2_ref, th_ref):
    for strip in range(st_ref.shape[1]):
        e1, e2, theta = _route_strip(st_ref[0, strip], st_ref[1, strip])
        e1_ref[strip] = e1
        e2_ref[strip] = e2
        th_ref[:, strip * LANES:(strip + 1) * LANES] = theta


def _route_strip(s1, s2):
    a = _top_sorted(s1, PEER_TOPK)
    b = _top_sorted(s2, PEER_TOPK)
    half = PEER_TOPK // 2
    cands = [a[0:half] + b[0:1], a[half:] + b[0:1], a[0:1] + b[half:]]
    for c in range(1, half):
        cands.append(a[0:half] + b[c:c + 1])
    cand = jnp.concatenate(cands, axis=0)
    work = cand
    for _ in range(PEER_TOPK):
        theta = jnp.max(work, axis=0, keepdims=True)
        work = jnp.where(work == theta, NEG_INF, work)
    top = a[0:1] + b[0:1]
    zsum = jnp.sum(jnp.where(cand >= theta, jnp.exp(cand - top), 0.0), axis=0, keepdims=True)
    return jnp.exp(s1 - a[0:1]), jnp.exp(s2 - b[0:1]) * (0.5 / zsum), theta


def _peer_route(st, tt):
    ns_all = st.shape[2]
    ns = tt // LANES
    tile = pl.BlockSpec((None, ns, N_KEYS, LANES), lambda h, i: (h, i, 0, 0))
    return pl.pallas_call(
        _peer_route_kernel,
        grid=(PEER_HEADS, ns_all // ns),
        in_specs=[pl.BlockSpec((None, 2, ns, N_KEYS, LANES), lambda h, i: (h, 0, i, 0, 0))],
        out_specs=[tile, tile, pl.BlockSpec((None, 1, tt), lambda h, i: (h, 0, i))],
        out_shape=[jax.ShapeDtypeStruct((PEER_HEADS, ns_all, N_KEYS, LANES), F32),
                   jax.ShapeDtypeStruct((PEER_HEADS, ns_all, N_KEYS, LANES), F32),
                   jax.ShapeDtypeStruct((PEER_HEADS, 1, ns_all * LANES), F32)],
        compiler_params=_params("parallel", "parallel"),
        name="peer_route",
    )(st)


def _pack_tables_kernel(u_ref, v_ref, upk_ref, vtpk_ref):
    upk_ref[...] = pltpu.bitcast(u_ref[...].astype(BF16), jnp.uint32)
    vtpk_ref[...] = pltpu.bitcast(v_ref[...].T.astype(BF16), jnp.uint32)


def _pack_tables(u, v, rows):
    n = u.shape[0]
    return pl.pallas_call(
        _pack_tables_kernel,
        grid=(n // rows,),
        in_specs=[pl.BlockSpec((rows, D_MODEL), lambda i: (i, 0)),
                  pl.BlockSpec((rows, D_MODEL), lambda i: (i, 0))],
        out_specs=[pl.BlockSpec((rows // 2, D_MODEL), lambda i: (i, 0)),
                   pl.BlockSpec((D_MODEL // 2, rows), lambda i: (0, i))],
        out_shape=[jax.ShapeDtypeStruct((n // 2, D_MODEL), jnp.uint32),
                   jax.ShapeDtypeStruct((D_MODEL // 2, n), jnp.uint32)],
        compiler_params=_params("parallel"),
        name="pack_tables",
    )(u, v)


def _peer_dense_kernel(h2t_ref, x1_ref, s1_ref, s2_ref, e1_ref, e2_ref, th_ref, u_ref, vt_ref, gf_ref,
                       y_ref, a0_scr, a1_scr, p0_scr, p1_scr, acc_scr, *, tt, ni):
    e = pl.program_id(1)
    n_strips = tt // LANES
    a_rows = ni * N_KEYS // n_strips
    d_rows = D_MODEL // n_strips

    @pl.when(e == 0)
    def _():
        for ref in (a0_scr, a1_scr, p0_scr, p1_scr, acc_scr):
            ref[...] = jnp.zeros_like(ref)

    def step(a_new, a_mid, p_old, p_mid):

        def per_strip(ls, carry):
            ar = pl.ds(pl.multiple_of(ls * a_rows, a_rows), a_rows)
            u_rows = pl.ds(pl.multiple_of(ls * (a_rows // 2), a_rows // 2), a_rows // 2)
            scores = _dot(pltpu.bitcast(u_ref[u_rows, :], BF16), h2t_ref[...])
            for s in range(n_strips):
                a_new[s, ar, :] = scores[:, s * LANES:(s + 1) * LANES]
            dr = pl.ds(pl.multiple_of(ls * d_rows, d_rows), d_rows)
            v_rows = pl.ds(pl.multiple_of(ls * (d_rows // 2), d_rows // 2), d_rows // 2)
            weighted = jnp.concatenate([p_old[s] for s in range(n_strips)], axis=1)
            acc_scr[dr, :] += _dot(pltpu.bitcast(vt_ref[v_rows, :], BF16), weighted)
            cols = pl.ds(pl.multiple_of(ls * LANES, LANES), LANES)
            for il in range(ni):
                rows = slice(il * N_KEYS, (il + 1) * N_KEYS)
                w = jnp.zeros((N_KEYS, LANES), F32)
                for hh in range(PEER_HEADS):
                    s1row = s1_ref[hh, ls, il:il + 1, :]
                    e1row = e1_ref[hh, ls, il:il + 1, :]
                    sel = (s1row + s2_ref[hh, ls]) >= th_ref[hh, :, cols]
                    w = w + jnp.where(sel, e2_ref[hh, ls], 0.0) * e1row
                a = a_mid[ls, rows, :]
                act = a * (1.0 + lax.erf(a * (2.0 ** -0.5)))
                p_mid[ls, rows, :] = (w * act).astype(BF16)
            return carry

        lax.fori_loop(0, n_strips, per_strip, 0)

    @pl.when(e % 2 == 0)
    def _():
        step(a0_scr, a1_scr, p0_scr, p1_scr)

    @pl.when(e % 2 == 1)
    def _():
        step(a1_scr, a0_scr, p1_scr, p0_scr)

    @pl.when(e == pl.num_programs(1) - 1)
    def _():
        y_ref[...] = _rms(x1_ref[...] + acc_scr[...].T, gf_ref[...])


def _peer_dense(h2t, x1, st, e1, e2, th, u_bf, vt_bf, gf, tt, ni):
    T = x1.shape[0]
    ne = N_KEYS // ni
    et = ni * N_KEYS
    ns = tt // LANES
    tok = lambda t, e: (t, 0)
    tile = lambda e, lag: jnp.clip(e - lag, 0, ne - 1)
    return pl.pallas_call(
        functools.partial(_peer_dense_kernel, tt=tt, ni=ni),
        grid=(T // tt, ne + 2),
        in_specs=[pl.BlockSpec((D_MODEL, tt), lambda t, e: (0, t)),
                  pl.BlockSpec((tt, D_MODEL), tok),
                  pl.BlockSpec((PEER_HEADS, None, ns, ni, LANES),
                               lambda t, e: (0, 0, t, tile(e, 1), 0)),
                  pl.BlockSpec((PEER_HEADS, None, ns, N_KEYS, LANES), lambda t, e: (0, 1, t, 0, 0)),
                  pl.BlockSpec((PEER_HEADS, ns, ni, LANES), lambda t, e: (0, t, tile(e, 1), 0)),
                  pl.BlockSpec((PEER_HEADS, ns, N_KEYS, LANES), lambda t, e: (0, t, 0, 0)),
                  pl.BlockSpec((PEER_HEADS, 1, tt), lambda t, e: (0, 0, t)),
                  pl.BlockSpec((et // 2, D_MODEL), lambda t, e: (tile(e, 0), 0)),
                  pl.BlockSpec((D_MODEL // 2, et), lambda t, e: (0, tile(e, 2))),
                  pl.BlockSpec((1, D_MODEL), lambda t, e: (0, 0))],
        out_specs=pl.BlockSpec((tt, D_MODEL), tok),
        out_shape=jax.ShapeDtypeStruct((T, D_MODEL), F32),
        scratch_shapes=[pltpu.VMEM((ns, et, LANES), F32),
                        pltpu.VMEM((ns, et, LANES), F32),
                        pltpu.VMEM((ns, et, LANES), BF16),
                        pltpu.VMEM((ns, et, LANES), BF16),
                        pltpu.VMEM((D_MODEL, tt), F32)],
        compiler_params=_params("parallel", "arbitrary"),
        name="peer_dense",
    )(h2t, x1, st, st, e1, e2, th, u_bf, vt_bf, gf)


def _rope_tables(pos):
    half = ROT_DIM // 2
    inv = jnp.power(ROPE_THETA, -jnp.arange(half, dtype=F32) * 2.0 / ROT_DIM)
    ang = pos.astype(F32)[:, None] * inv[None, :]
    cos, sin = jnp.cos(ang), jnp.sin(ang)
    n = pos.shape[0]
    pad = jnp.zeros((n, DA_HD - ROT_DIM), F32)
    zero = jnp.zeros((n, half), F32)
    cos_t = jnp.concatenate([cos, cos, pad + 1.0], axis=1)
    sin_a = jnp.concatenate([-sin, zero, pad], axis=1)
    sin_b = jnp.concatenate([zero, sin, pad], axis=1)
    return tuple(jnp.tile(t, (1, LANES // DA_HD)) for t in (cos_t, sin_a, sin_b))


def _trunk(x, pos, past_k, past_v, s0, chunk, wts, layer, tm, tq, gla_rows, tt, ni):
    B, L, _ = x.shape
    T = B * L
    lam_init = 0.8 - 0.6 * math.exp(-0.3 * layer)
    x2d = x.reshape(T, D_MODEL)
    cos_t, sin_a, sin_b = _rope_tables(pos)
    if L < tm:
        cos_t, sin_a, sin_b = (jnp.tile(t, (tm // L, 1)) for t in (cos_t, sin_a, sin_b))

    q_bf, k32, k_bf, v32, v_bf, siga = _da_proj(
        x2d, wts["g1"], wts["w_qkv"], wts["w_ga"], cos_t, sin_a, sin_b, tm,
        transpose_v=past_k is None)
    if past_k is None:
        oda = _attn_prompt(wts["lam4"], q_bf, k_bf, v_bf, siga, wts["subg"], B, L, tq, lam_init)
    else:
        oda = _attn_sample(wts["lam4"], q_bf, past_k, past_v, k_bf, v_bf, siga, wts["subg"],
                           B, L, lam_init)
    merged, s_fin = _gla(x2d, wts["g1"], wts["w_g"], wts["w_lr"], wts["w_a"], wts["b_a"],
                         wts["gla_g"], s0, oda, B, L, gla_rows, chunk)
    x1, h2t, st = _peer_query(x2d, merged, wts["w_out"], wts["g2"], wts["wq"], wts["subkeys"], tm)
    st = st.reshape(PEER_HEADS, 2, T // LANES, N_KEYS, LANES)
    e1, e2, th = _peer_route(st, tt)
    y = _peer_dense(h2t, x1, st, e1, e2, th, wts["u"], wts["vt"], wts["gf"], tt, ni)
    return (y.reshape(B, L, D_MODEL), k32.reshape(1, B, L, DA_HEADS, 2, DA_HD),
            v32.reshape(1, B, L, DA_HEADS, DA_VD), s_fin[None])


def kernel(x_prompt, x_sample, cache_da_k, cache_da_v, state_gla, norm1_g, w_in, da_lambda_q1,
           da_lambda_k1, da_lambda_q2, da_lambda_k2, da_subln_g, gla_w_alpha, gla_b_alpha,
           gla_norm_g, w_out, norm2_g, peer_wq, peer_subkeys, peer_u, peer_v, final_norm_g):
    layer = 0
    B, L, _ = x_prompt.shape
    SB, SL, _ = x_sample.shape
    P = cache_da_k.shape[2]

    o_dq, o_gq = 0, 3 * D_MODEL
    o_gr = o_gq + 2 * GLA_K + GLA_V
    o_lr = o_gr + GLA_V
    o_ga = o_lr + GLA_GATE_RANK
    o_gb = o_ga + D_MODEL
    w = w_in[layer]
    pad_rank = LANES - GLA_GATE_RANK
    wts = {
        "g1": norm1_g[layer][None, :],
        "w_qkv": w[:, o_dq:o_gq].astype(BF16),
        "w_ga": w[:, o_ga:o_gb].astype(BF16),
        "w_g": jnp.concatenate([w[:, o_gq:o_lr], w[:, o_gb:]], axis=1).astype(BF16),
        "w_lr": jnp.pad(w[:, o_lr:o_ga], ((0, 0), (0, pad_rank))).astype(BF16),
        "w_a": jnp.pad(gla_w_alpha[layer], ((0, pad_rank), (0, 0))).astype(BF16),
        "b_a": gla_b_alpha[layer][None, :],
        "gla_g": gla_norm_g[layer][None, :],
        "lam4": jnp.stack([da_lambda_q1[layer], da_lambda_k1[layer],
                           da_lambda_q2[layer], da_lambda_k2[layer]]),
        "subg": da_subln_g[layer][None, :],
        "w_out": w_out[layer].astype(BF16),
        "g2": norm2_g[layer][None, :],
        "wq": peer_wq[layer].astype(BF16),
        "subkeys": peer_subkeys[layer].reshape(2 * PEER_HEADS, N_KEYS, PK_DIM).astype(BF16),
        "gf": final_norm_g[None, :],
    }
    wts["u"], wts["vt"] = _pack_tables(peer_u[layer], peer_v[layer], rows=512)

    pos_p = jnp.arange(L, dtype=F32)
    pos_s = P + jnp.arange(SL, dtype=F32)
    zeros_state = jnp.zeros((B, GLA_HEADS, GLA_HK, GLA_HV), F32)
    tm_p = min(256, B * L)
    y_p, k_p, v_p, s_p = _trunk(x_prompt, pos_p, None, None, zeros_state, CHUNK, wts, layer,
                                tm=tm_p, tq=min(256, L), gla_rows=min(256, L),
                                tt=min(512, B * L), ni=8)
    past_k = cache_da_k[layer].reshape(SB, P, D_MODEL)
    past_v = cache_da_v[layer].reshape(SB, P, D_MODEL)
    y_s, k_s, v_s, s_s = _trunk(x_sample, pos_s, past_k, past_v, state_gla[layer], SL, wts, layer,
                                tm=SB * SL, tq=SL, gla_rows=SL, tt=SB * SL, ni=8)
    return (y_p, y_s, k_p, v_p, s_p, k_s, v_s, s_s)
```
